```python
import math
import jax, jax.numpy as jnp
from jax import lax
import numpy as np

D_MODEL = 1024
BATCH = 16
SEQ = 2048
DEPTH = 4
DEC_BATCH = 16
DEC_SEQ = 4096
PAST_LEN = 128

N_DIR = 2
D_RWKV = 512
RWKV_HEAD = 64
RWKV_HEADS = D_RWKV // RWKV_HEAD
LORA_W = 64
LORA_A = 64
LORA_G = 128
D_S5 = 256
S5_GROUP = 16
S5_GROUPS = D_S5 // S5_GROUP
S5_STATE = 64
D_HYENA = 256
HYENA_ORDER = 2
HYENA_EMB = 33
HYENA_BANDS = (HYENA_EMB - 1) // 2
HYENA_FILTER_WIDTH = 64
HYENA_FAST_DECAY = 0.3
HYENA_SLOW_DECAY = 1.5
HYENA_TARGET = 1e-2
D_MIX = D_RWKV + D_S5 + D_HYENA
RWKV_COLS = 3 * D_RWKV + N_DIR * (LORA_W + LORA_A) + LORA_G
HYENA_COLS = (HYENA_ORDER + 1) * D_HYENA
D_IN = RWKV_COLS + D_S5 + HYENA_COLS
N_EXPERTS = 16
EC_CAPACITY_FACTOR = 2
D_EXPERT = 1024
EPS_NORM = 1e-6
EPS_GN = 64e-5

kernel_name = 'hybrid_bidir_rwkv7_s5_hyena_ec_moe'


def rmsnorm(x, g):
    xf = x.astype(jnp.float32)
    y = xf * lax.rsqrt(jnp.mean(xf * xf, axis=-1, keepdims=True) + EPS_NORM)
    return (y * g.astype(jnp.float32)).astype(x.dtype)


def shift_prev(z):
    return jnp.pad(z[:, :-1], ((0, 0), (1, 0), (0, 0)))


def shift_next(z):
    return jnp.pad(z[:, 1:], ((0, 0), (0, 1), (0, 0)))


def rwkv7_scan(r, w, k, v, kk, a, reverse):
    bsz = r.shape[0]

    def step(S, inp):
        r_t, w_t, k_t, v_t, kk_t, a_t = inp
        sa = jnp.einsum('bhvk,bhk->bhv', S, kk_t)
        S = (S * w_t[:, :, None, :]
             - sa[..., None] * (kk_t * a_t)[:, :, None, :]
             + v_t[..., None] * k_t[:, :, None, :])
        return S, jnp.einsum('bhvk,bhk->bhv', S, r_t)

    xs = tuple(jnp.moveaxis(t, 1, 0) for t in (r, w, k, v, kk, a))
    S0 = jnp.zeros((bsz, RWKV_HEADS, RWKV_HEAD, RWKV_HEAD), jnp.float32)
    _, ys = lax.scan(step, S0, xs, reverse=reverse)
    return jnp.moveaxis(ys, 0, 1)


def rwkv7_mixer(z, mu_prev, mu_next, w0, w_up, a0, a_up, g_up, k_k, k_a, r_k, ln_w, ln_b):
    z = z.astype(jnp.float32)
    bsz, L, _ = z.shape
    z = z + mu_prev * (shift_prev(z) - z) + mu_next * (shift_next(z) - z)
    c1, c2, c3 = D_RWKV, 2 * D_RWKV, 3 * D_RWKV
    c4 = c3 + N_DIR * LORA_W
    c5 = c4 + N_DIR * LORA_A
    r, k, v, wd, ad, gd = jnp.split(z, [c1, c2, c3, c4, c5], axis=-1)
    wd = wd.reshape(bsz, L, N_DIR, LORA_W)
    ad = ad.reshape(bsz, L, N_DIR, LORA_A)
    w_raw = w0 + jnp.einsum('bldr,drc->bldc', jnp.tanh(wd), w_up)
    decay = jnp.exp(-jnp.exp(-jax.nn.softplus(-w_raw) - 0.5))
    a = jax.nn.sigmoid(a0 + jnp.einsum('bldr,drc->bldc', ad, a_up))
    g = jax.nn.sigmoid(gd) @ g_up
    heads = lambda t: t.reshape(bsz, L, RWKV_HEADS, RWKV_HEAD)
    kk = heads(k * k_k)
    kk = kk / jnp.maximum(jnp.sqrt(jnp.sum(kk * kk, axis=-1, keepdims=True)), 1e-12)
    r_h, v_h = heads(r), heads(v)
    y = 0.0
    k_sum = 0.0
    for d in range(N_DIR):
        a_d = a[:, :, d]
        k_d = k * (1.0 + (a_d - 1.0) * k_a)
        y = y + rwkv7_scan(r_h, heads(decay[:, :, d]), heads(k_d), v_h, kk, heads(a_d), reverse=(d == 1))
        k_sum = k_sum + k_d
    mean = jnp.mean(y, axis=-1, keepdims=True)
    var = jnp.mean(jnp.square(y - mean), axis=-1, keepdims=True)
    y = ((y - mean) * lax.rsqrt(var + EPS_GN)).reshape(bsz, L, D_RWKV) * ln_w + ln_b
    bonus = (jnp.sum(r_h * heads(k_sum) * r_k, axis=-1, keepdims=True) * v_h).reshape(bsz, L, D_RWKV)
    return (y + bonus) * g


def complex_linear_combine(e1, e2):
    a1r, a1i, b1r, b1i = e1
    a2r, a2i, b2r, b2i = e2
    return (a2r * a1r - a2i * a1i,
            a2r * a1i + a2i * a1r,
            a2r * b1r - a2i * b1i + b2r,
            a2r * b1i + a2i * b1r + b2i)


def s5_mixer(u, lam_re, lam_im, log_dt, b_re, b_im, c_re, c_im, d_skip, w_glu, norm_g):
    u = u.astype(jnp.float32)
    bsz, L, _ = u.shape
    ug = u.reshape(bsz, L, S5_GROUPS, S5_GROUP)
    bu_re = jnp.einsum('blgh,gph->blgp', ug, b_re)
    bu_im = jnp.einsum('blgh,gph->blgp', ug, b_im)
    x_re = 0.0
    x_im = 0.0
    for d in range(N_DIR):
        l_re = jnp.minimum(lam_re[d], -1e-4)
        l_im = lam_im[d]
        dt = jnp.exp(log_dt[d])[:, None]
        mag = jnp.exp(l_re * dt)
        ab_re = mag * jnp.cos(l_im * dt)
        ab_im = mag * jnp.sin(l_im * dt)
        den = l_re * l_re + l_im * l_im
        n_re = ab_re - 1.0
        n_im = ab_im
        f_re = (n_re * l_re + n_im * l_im) / den
        f_im = (n_im * l_re - n_re * l_im) / den
        in_re = f_re * bu_re - f_im * bu_im
        in_im = f_re * bu_im + f_im * bu_re
        a_re = jnp.broadcast_to(ab_re[None, None], (1, L, S5_GROUPS, S5_STATE))
        a_im = jnp.broadcast_to(ab_im[None, None], (1, L, S5_GROUPS, S5_STATE))
        _, _, s_re, s_im = lax.associative_scan(
            complex_linear_combine, (a_re, a_im, in_re, in_im), reverse=(d == 1), axis=1)
        x_re = x_re + s_re
        x_im = x_im + s_im
    y = jnp.einsum('blgp,ghp->blgh', x_re, c_re) - jnp.einsum('blgp,ghp->blgh', x_im, c_im)
    y = y.reshape(bsz, L, D_S5) + d_skip * u
    h = jax.nn.gelu(y)
    out = h * jax.nn.sigmoid(h @ w_glu)
    return rmsnorm(out, norm_g)


def hyena_filter_spectrum(L, f_w1, f_b1, f_w2, f_b2, f_w3, f_b3, f_freq):
    t = jnp.linspace(0.0, 1.0, L, dtype=jnp.float32)[:, None]
    w = (2.0 * math.pi / L) * jnp.arange(L, dtype=jnp.float32)[:, None]
    f = jnp.linspace(1e-4, HYENA_BANDS - 1, HYENA_BANDS, dtype=jnp.float32)[None, :]
    feats = jnp.concatenate([t, jnp.cos(f * w), -jnp.sin(f * w)], axis=-1)
    h = jnp.sin(f_freq * (feats @ f_w1 + f_b1))
    h = jnp.sin(f_freq * (h @ f_w2 + f_b2))
    h = (h @ f_w3 + f_b3).reshape(L, HYENA_ORDER, N_DIR, D_HYENA)
    deltas = jnp.linspace(math.log(HYENA_TARGET) / HYENA_FAST_DECAY,
                          math.log(HYENA_TARGET) / HYENA_SLOW_DECAY, D_HYENA, dtype=jnp.float32)
    h = h * jnp.exp(-t[:, :, None, None] * jnp.abs(deltas))
    k = jnp.concatenate([h[:, :, 0],
                         jnp.zeros((1, HYENA_ORDER, D_HYENA), h.dtype),
                         h[:0:-1, :, 1]], axis=0)
    k = k / jnp.sum(jnp.abs(k), axis=0, keepdims=True)
    return jnp.fft.rfft(k, n=2 * L, axis=0)


def hyena_mixer(z, conv_w, conv_b, kf, hy_bias, norm_g):
    z = z.astype(jnp.float32)
    L = z.shape[1]
    z = conv_w[0] * shift_prev(z) + conv_w[1] * z + conv_w[2] * shift_next(z) + conv_b
    v, x1, x2 = jnp.split(z, HYENA_ORDER + 1, axis=-1)

    def longconv(u, o):
        uf = jnp.fft.rfft(u, n=2 * L, axis=1)
        return jnp.fft.irfft(uf * kf[:, o], n=2 * L, axis=1)[:, :L] + hy_bias[o] * u

    z1 = x1 * longconv(v, 0)
    y = x2 * longconv(z1, 1)
    return rmsnorm(y, norm_g)


def expert_choice_moe(x, w_router, w_gate, w_up, w_down):
    bsz, L, D = x.shape
    T = bsz * L
    cap = EC_CAPACITY_FACTOR * T // N_EXPERTS
    xt = x.reshape(T, D)
    aff = jax.nn.softmax((xt @ w_router).astype(jnp.float32), axis=-1)
    gate, idx = lax.top_k(aff.T, cap)
    xe = xt[idx]
    h = jax.nn.silu(jnp.einsum('ecd,edf->ecf', xe, w_gate)) * jnp.einsum('ecd,edf->ecf', xe, w_up)
    ye = jnp.einsum('ecf,efd->ecd', h, w_down) * gate[..., None].astype(x.dtype)
    out = jnp.zeros_like(xt).at[idx.reshape(-1)].add(ye.reshape(-1, D))
    return out.reshape(bsz, L, D)


def trunk(x, p, final_norm_g):
    L = x.shape[1]
    for i in range(DEPTH):
        h = rmsnorm(x, p['norm_mix_g'][i])
        z = h @ p['w_in'][i]
        z_rwkv, z_s5, z_hy = jnp.split(z, [RWKV_COLS, RWKV_COLS + D_S5], axis=-1)
        y_rwkv = rwkv7_mixer(z_rwkv, p['rwkv_mu_prev'][i], p['rwkv_mu_next'][i], p['rwkv_w0'][i],
                             p['rwkv_w_up'][i], p['rwkv_a0'][i], p['rwkv_a_up'][i], p['rwkv_g_up'][i],
                             p['rwkv_k_k'][i], p['rwkv_k_a'][i], p['rwkv_r_k'][i],
                             p['rwkv_ln_w'][i], p['rwkv_ln_b'][i])
        y_s5 = s5_mixer(z_s5, p['s5_lam_re'][i], p['s5_lam_im'][i], p['s5_log_dt'][i],
                        p['s5_b_re'][i], p['s5_b_im'][i], p['s5_c_re'][i], p['s5_c_im'][i],
                        p['s5_d'][i], p['s5_w_glu'][i], p['s5_norm_g'][i])
        kf = hyena_filter_spectrum(L, p['hy_f_w1'][i], p['hy_f_b1'][i], p['hy_f_w2'][i], p['hy_f_b2'][i],
                                   p['hy_f_w3'][i], p['hy_f_b3'][i], p['hy_f_freq'][i])
        y_hy = hyena_mixer(z_hy, p['hy_conv_w'][i], p['hy_conv_b'][i], kf, p['hy_bias'][i], p['hy_norm_g'][i])
        y = jnp.concatenate([y_rwkv, y_s5, y_hy], axis=-1).astype(x.dtype)
        x = x + y @ p['w_out'][i]
        x = x + expert_choice_moe(rmsnorm(x, p['norm_ffn_g'][i]), p['moe_w_router'][i],
                                  p['moe_w_gate'][i], p['moe_w_up'][i], p['moe_w_down'][i])
    return rmsnorm(x, final_norm_g)


def setup_inputs(seed: int = 0) -> dict:
    key = jax.random.key(seed)
    keys = jax.random.split(key, 64)
    ctr = [0]

    def nk():
        ctr[0] += 1
        return keys[ctr[0] - 1]

    def nrm(shape, scale=1.0):
        return scale * jax.random.normal(nk(), shape, jnp.float32)

    def unif(shape, lo, hi):
        return jax.random.uniform(nk(), shape, jnp.float32, lo, hi)

    def gain(shape):
        return 1.0 + nrm(shape, 0.02)

    inp = {}
    inp['x_prompt'] = nrm((BATCH, SEQ, D_MODEL))
    inp['x_sample'] = nrm((DEC_BATCH, DEC_SEQ, D_MODEL))
    inp['norm_mix_g'] = gain((DEPTH, D_MODEL))
    inp['w_in'] = nrm((DEPTH, D_MODEL, D_IN), D_MODEL ** -0.5)
    inp['rwkv_mu_prev'] = unif((DEPTH, RWKV_COLS), 0.0, 0.5)
    inp['rwkv_mu_next'] = unif((DEPTH, RWKV_COLS), 0.0, 0.5)
    inp['rwkv_w0'] = unif((DEPTH, N_DIR, D_RWKV), -6.0, 0.0)
    inp['rwkv_w_up'] = nrm((DEPTH, N_DIR, LORA_W, D_RWKV), 0.5 * LORA_W ** -0.5)
    inp['rwkv_a0'] = nrm((DEPTH, N_DIR, D_RWKV), 0.1)
    inp['rwkv_a_up'] = nrm((DEPTH, N_DIR, LORA_A, D_RWKV), LORA_A ** -0.5)
    inp['rwkv_g_up'] = nrm((DEPTH, LORA_G, D_RWKV), LORA_G ** -0.5)
    inp['rwkv_k_k'] = 0.85 + nrm((DEPTH, D_RWKV), 0.05)
    inp['rwkv_k_a'] = 1.0 + nrm((DEPTH, D_RWKV), 0.05)
    inp['rwkv_r_k'] = nrm((DEPTH, RWKV_HEADS, RWKV_HEAD), 0.1)
    inp['rwkv_ln_w'] = gain((DEPTH, D_RWKV))
    inp['rwkv_ln_b'] = nrm((DEPTH, D_RWKV), 0.02)
    inp['s5_lam_re'] = -0.5 + nrm((DEPTH, N_DIR, S5_GROUPS, S5_STATE), 0.01)
    inp['s5_lam_im'] = math.pi * jnp.arange(S5_STATE, dtype=jnp.float32) + nrm((DEPTH, N_DIR, S5_GROUPS, S5_STATE), 0.01)
    inp['s5_log_dt'] = unif((DEPTH, N_DIR, S5_GROUPS), math.log(0.001), math.log(0.1))
    inp['s5_b_re'] = nrm((DEPTH, S5_GROUPS, S5_STATE, S5_GROUP), (2 * S5_GROUP) ** -0.5)
    inp['s5_b_im'] = nrm((DEPTH, S5_GROUPS, S5_STATE, S5_GROUP), (2 * S5_GROUP) ** -0.5)
    inp['s5_c_re'] = nrm((DEPTH, S5_GROUPS, S5_GROUP, S5_STATE), (2 * S5_STATE) ** -0.5)
    inp['s5_c_im'] = nrm((DEPTH, S5_GROUPS, S5_GROUP, S5_STATE), (2 * S5_STATE) ** -0.5)
    inp['s5_d'] = nrm((DEPTH, D_S5))
    inp['s5_w_glu'] = nrm((DEPTH, D_S5, D_S5), D_S5 ** -0.5)
    inp['s5_norm_g'] = gain((DEPTH, D_S5))
    inp['hy_conv_w'] = nrm((DEPTH, 3, HYENA_COLS), 3 ** -0.5)
    inp['hy_conv_b'] = nrm((DEPTH, HYENA_COLS), 0.02)
    inp['hy_f_w1'] = nrm((DEPTH, HYENA_EMB, HYENA_FILTER_WIDTH), HYENA_EMB ** -0.5)
    inp['hy_f_b1'] = nrm((DEPTH, HYENA_FILTER_WIDTH), 0.02)
    inp['hy_f_w2'] = nrm((DEPTH, HYENA_FILTER_WIDTH, HYENA_FILTER_WIDTH), HYENA_FILTER_WIDTH ** -0.5)
    inp['hy_f_b2'] = nrm((DEPTH, HYENA_FILTER_WIDTH), 0.02)
    inp['hy_f_w3'] = nrm((DEPTH, HYENA_FILTER_WIDTH, HYENA_ORDER * N_DIR * D_HYENA), HYENA_FILTER_WIDTH ** -0.5)
    inp['hy_f_b3'] = nrm((DEPTH, HYENA_ORDER * N_DIR * D_HYENA), 0.02)
    inp['hy_f_freq'] = gain((DEPTH, HYENA_FILTER_WIDTH))
    inp['hy_bias'] = nrm((DEPTH, HYENA_ORDER, D_HYENA))
    inp['hy_norm_g'] = gain((DEPTH, D_HYENA))
    inp['w_out'] = nrm((DEPTH, D_MIX, D_MODEL), D_MIX ** -0.5)
    inp['norm_ffn_g'] = gain((DEPTH, D_MODEL))
    inp['moe_w_router'] = nrm((DEPTH, D_MODEL, N_EXPERTS), D_MODEL ** -0.5)
    inp['moe_w_gate'] = nrm((DEPTH, N_EXPERTS, D_MODEL, D_EXPERT), D_MODEL ** -0.5)
    inp['moe_w_up'] = nrm((DEPTH, N_EXPERTS, D_MODEL, D_EXPERT), D_MODEL ** -0.5)
    inp['moe_w_down'] = nrm((DEPTH, N_EXPERTS, D_EXPERT, D_MODEL), D_EXPERT ** -0.5)
    inp['final_norm_g'] = gain((D_MODEL,))
    return inp


def reference(x_prompt, x_sample, norm_mix_g, w_in, rwkv_mu_prev, rwkv_mu_next, rwkv_w0, rwkv_w_up,
              rwkv_a0, rwkv_a_up, rwkv_g_up, rwkv_k_k, rwkv_k_a, rwkv_r_k, rwkv_ln_w, rwkv_ln_b,
              s5_lam_re, s5_lam_im, s5_log_dt, s5_b_re, s5_b_im, s5_c_re, s5_c_im, s5_d, s5_w_glu,
              s5_norm_g, hy_conv_w, hy_conv_b, hy_f_w1, hy_f_b1, hy_f_w2, hy_f_b2, hy_f_w3, hy_f_b3,
              hy_f_freq, hy_bias, hy_norm_g, w_out, norm_ffn_g, moe_w_router, moe_w_gate, moe_w_up,
              moe_w_down, final_norm_g):
    p = dict(norm_mix_g=norm_mix_g, w_in=w_in, rwkv_mu_prev=rwkv_mu_prev, rwkv_mu_next=rwkv_mu_next,
             rwkv_w0=rwkv_w0, rwkv_w_up=rwkv_w_up, rwkv_a0=rwkv_a0, rwkv_a_up=rwkv_a_up,
             rwkv_g_up=rwkv_g_up, rwkv_k_k=rwkv_k_k, rwkv_k_a=rwkv_k_a, rwkv_r_k=rwkv_r_k,
             rwkv_ln_w=rwkv_ln_w, rwkv_ln_b=rwkv_ln_b, s5_lam_re=s5_lam_re, s5_lam_im=s5_lam_im,
             s5_log_dt=s5_log_dt, s5_b_re=s5_b_re, s5_b_im=s5_b_im, s5_c_re=s5_c_re, s5_c_im=s5_c_im,
             s5_d=s5_d, s5_w_glu=s5_w_glu, s5_norm_g=s5_norm_g, hy_conv_w=hy_conv_w, hy_conv_b=hy_conv_b,
             hy_f_w1=hy_f_w1, hy_f_b1=hy_f_b1, hy_f_w2=hy_f_w2, hy_f_b2=hy_f_b2, hy_f_w3=hy_f_w3,
             hy_f_b3=hy_f_b3, hy_f_freq=hy_f_freq, hy_bias=hy_bias, hy_norm_g=hy_norm_g, w_out=w_out,
             norm_ffn_g=norm_ffn_g, moe_w_router=moe_w_router, moe_w_gate=moe_w_gate,
             moe_w_up=moe_w_up, moe_w_down=moe_w_down)
    y_prompt = trunk(x_prompt, p, final_norm_g)
    y_sample = trunk(x_sample, p, final_norm_g)
    return (y_prompt, y_sample)
```

```python
import functools
import math

import jax
import jax.numpy as jnp
from jax import lax
from jax.experimental import pallas as pl
from jax.experimental.pallas import tpu as pltpu

F32 = jnp.float32
BF16 = jnp.bfloat16
I32 = jnp.int32
SDS = jax.ShapeDtypeStruct

EPS_NORM = 1e-6
EPS_GN = 64e-5
HYENA_FAST_DECAY = 0.3
HYENA_SLOW_DECAY = 1.5
HYENA_TARGET = 1e-2
EC_CAPACITY_FACTOR = 2

V7X_LANES = 128
V7X_SUBLANES = 8
V7X_VMEM_LIMIT_BYTES = 56 * 1024 * 1024

RWKV_CHUNK = 64
DFT_INNER = 64


def _cparams(sem, vmem=V7X_VMEM_LIMIT_BYTES):
    return pltpu.CompilerParams(dimension_semantics=sem, vmem_limit_bytes=vmem)


def _bdot(a, b):
    return jnp.dot(a.astype(BF16), b.astype(BF16), preferred_element_type=F32)


def _bdot_t(a, b):
    return lax.dot_general(a.astype(BF16), b.astype(BF16), (((1,), (1,)), ((), ())),
                           preferred_element_type=F32)


def _split2(x):
    hi = x.astype(BF16)
    lo = (x - hi.astype(F32)).astype(BF16)
    return hi, lo


def _split3(x):
    hi = x.astype(BF16)
    r1 = x - hi.astype(F32)
    mid = r1.astype(BF16)
    lo = (r1 - mid.astype(F32)).astype(BF16)
    return hi, mid, lo


def _dot3(a, b):
    ah, al = _split2(a)
    bh, bl = _split2(b)
    d = functools.partial(jnp.dot, preferred_element_type=F32)
    return d(ah, bh) + d(ah, bl) + d(al, bh)


def _dot3_t(a, b):
    ah, al = _split2(a)
    bh, bl = _split2(b)
    d = lambda x, y: lax.dot_general(x, y, (((1,), (1,)), ((), ())), preferred_element_type=F32)
    return d(ah, bh) + d(ah, bl) + d(al, bh)


def _dot_exact_rhs(m01, x):
    xh, xl = _split2(x)
    d = functools.partial(jnp.dot, preferred_element_type=F32)
    return d(m01, xh) + d(m01, xl)


def _dot_exact_lhs(x, m01):
    xh, xl = _split2(x)
    d = functools.partial(jnp.dot, preferred_element_type=F32)
    return d(xh, m01) + d(xl, m01)


def _rms(x, g):
    return x * lax.rsqrt(jnp.mean(x * x, axis=-1, keepdims=True) + EPS_NORM) * g


def _sigmoid(x):
    return 1.0 / (1.0 + jnp.exp(-x))


def _softplus(x):
    return jnp.maximum(x, 0.0) + jnp.log(1.0 + jnp.exp(-jnp.abs(x)))


def _shift_rows(z, prev_row, next_row):
    n = z.shape[0]
    rows = lax.broadcasted_iota(I32, z.shape, 0)
    zprev = jnp.where(rows == 0, prev_row, pltpu.roll(z, 1, 0))
    znext = jnp.where(rows == n - 1, next_row, pltpu.roll(z, n - 1, 0))
    return zprev, znext


def _halo_specs(tm, cols, nl, total_rows):
    r8 = tm // V7X_SUBLANES
    last = total_rows // V7X_SUBLANES - 1
    prev = pl.BlockSpec((V7X_SUBLANES, cols), lambda b, l: (jnp.maximum((b * nl + l) * r8 - 1, 0), 0))
    nxt = pl.BlockSpec((V7X_SUBLANES, cols), lambda b, l: (jnp.minimum((b * nl + l + 1) * r8, last), 0))
    return prev, nxt


def _halo_rows(zp_ref, zn_ref):
    l = pl.program_id(1)
    nl = pl.num_programs(1)
    prev_row = jnp.where(l == 0, 0.0, zp_ref[V7X_SUBLANES - 1:V7X_SUBLANES, :])
    next_row = jnp.where(l == nl - 1, 0.0, zn_ref[0:1, :])
    return prev_row, next_row


def _inproj_kernel(x_ref, g_ref, w_ref, zr_ref, zs_ref, zh_ref, *, c_r, c_s):
    h = _rms(x_ref[...], g_ref[...])
    z = _bdot(h, w_ref[...])
    zr_ref[...] = z[:, :c_r]
    zs_ref[...] = z[:, c_r:c_r + c_s]
    zh_ref[...] = z[:, c_r + c_s:]


def _in_proj(x, B, L, g, w, c_r, c_s, c_h, tm=256):
    T, D = x.shape
    nl = L // tm
    return pl.pallas_call(
        functools.partial(_inproj_kernel, c_r=c_r, c_s=c_s),
        grid=(B, nl),
        in_specs=[pl.BlockSpec((tm, D), lambda b, l: (b * nl + l, 0)),
                  pl.BlockSpec((1, D), lambda b, l: (0, 0)),
                  pl.BlockSpec((D, c_r + c_s + c_h), lambda b, l: (0, 0))],
        out_specs=[pl.BlockSpec((tm, c_r), lambda b, l: (b * nl + l, 0)),
                   pl.BlockSpec((tm, c_s), lambda b, l: (l, b)),
                   pl.BlockSpec((tm, c_h), lambda b, l: (b * nl + l, 0))],
        out_shape=[SDS((T, c_r), F32), SDS((L, B * c_s), F32), SDS((T, c_h), F32)],
        compiler_params=_cparams(("parallel", "parallel")), name="in_proj")(x, g, w)


def _rwkv_pre_kernel(z_ref, zp_ref, zn_ref, mup_ref, mun_ref, w0_ref, wup_ref, a0_ref, aup_ref, gup_ref,
                     kk_ref, ka_ref, hs_ref,
                     r_o, v_o, kkn_o, g_o, ks_o, lw_o, kd_o, bd_o, *, dr):
    z = z_ref[...]
    prev_row, next_row = _halo_rows(zp_ref, zn_ref)
    zprev, znext = _shift_rows(z, prev_row, next_row)
    zs = z + mup_ref[...] * (zprev - z) + mun_ref[...] * (znext - z)
    r = zs[:, 0:dr]
    k = zs[:, dr:2 * dr]
    v = zs[:, 2 * dr:3 * dr]
    c3 = 3 * dr
    wd = zs[:, c3:c3 + 128]
    ad = zs[:, c3 + 128:c3 + 256]
    gd = zs[:, c3 + 256:c3 + 384]
    w_raw = w0_ref[...] + _bdot(jnp.tanh(wd), wup_ref[...])
    lw = -jnp.exp(-_softplus(-w_raw) - 0.5)
    a = _sigmoid(a0_ref[...] + _bdot(ad, aup_ref[...]))
    g = _bdot(_sigmoid(gd), gup_ref[...])
    kk = k * kk_ref[...]
    ss = _dot_exact_lhs(kk * kk, hs_ref[...])
    kk = kk / jnp.maximum(jnp.sqrt(ss), 1e-12)
    ka = ka_ref[...]
    r_o[...] = r
    v_o[...] = v
    kkn_o[...] = kk
    g_o[...] = g
    ksum = None
    for d in range(2):
        a_d = a[:, d * dr:(d + 1) * dr]
        k_d = k * (1.0 + (a_d - 1.0) * ka)
        lw_o[d] = lw[:, d * dr:(d + 1) * dr]
        kd_o[d] = k_d
        bd_o[d] = kk * a_d
        ksum = k_d if ksum is None else ksum + k_d
    ks_o[...] = ksum


def _rwkv_pre(z, B, L, pw, tm=256):
    T, cr = z.shape
    dr = pw["dr"]
    nl = L // tm
    row = lambda b, l: (b * nl + l, 0)
    const = lambda b, l: (0, 0)
    prev, nxt = _halo_specs(tm, cr, nl, T)
    full = lambda a: pl.BlockSpec(a.shape, const)
    params = [pw["mu_prev"], pw["mu_next"], pw["w0"], pw["wup"], pw["a0"], pw["aup"], pw["gup"],
              pw["k_k"], pw["k_a"], pw["hsum"]]
    o1 = pl.BlockSpec((tm, dr), row)
    o2 = pl.BlockSpec((2, tm, dr), lambda b, l: (0, b * nl + l, 0))
    return pl.pallas_call(
        functools.partial(_rwkv_pre_kernel, dr=dr),
        grid=(B, nl),
        in_specs=[pl.BlockSpec((tm, cr), row), prev, nxt] + [full(a) for a in params],
        out_specs=[o1] * 5 + [o2] * 3,
        out_shape=[SDS((T, dr), F32)] * 5 + [SDS((2, T, dr), F32)] * 3,
        compiler_params=_cparams(("parallel", "parallel")), name="rwkv_pre")(z, z, z, *params)


def _rwkv_chunk_kernel(r_ref, v_ref, kk_ref, lw_ref, kd_ref, bd_ref, y_ref, s_ref, *, heads, hd):
    d = pl.program_id(0)
    c = pl.program_id(2)
    C = r_ref.shape[0]

    @pl.when(c == 0)
    def _():
        s_ref[...] = jnp.zeros_like(s_ref)

    ti = lax.broadcasted_iota(I32, (C, C), 0)
    si = lax.broadcasted_iota(I32, (C, C), 1)
    sgn = jnp.where(d == 0, 1, -1)
    dd = (si - ti) * sgn
    incl = dd <= 0
    strict = dd < 0
    incl_f = jnp.where(incl, 1.0, 0.0).astype(BF16)
    eye = jnp.where(dd == 0, 1.0, 0.0)

    lw = lw_ref[...]
    cum = _dot_exact_rhs(incl_f, lw)
    tot = jnp.sum(lw, axis=0, keepdims=True)
    p_incl = jnp.exp(cum)
    p_excl = jnp.exp(cum - lw)
    p_inv = jnp.exp(-cum)
    p_end = jnp.exp(tot - cum)
    p_tot = jnp.exp(tot)
    kd = kd_ref[...]
    bd = bd_ref[...]
    rt = r_ref[...] * p_incl
    at = kk_ref[...] * p_excl
    kt = kd * p_inv
    bt = bd * p_inv
    kbar = kd * p_end
    bbar = bd * p_end
    v = v_ref[...]

    for h in range(heads):
        sl = slice(h * hd, (h + 1) * hd)
        rt_h, at_h, kt_h, bt_h, v_h = rt[:, sl], at[:, sl], kt[:, sl], bt[:, sl], v[:, sl]
        S = s_ref[h]
        a_ab = jnp.where(strict, _dot3_t(at_h, bt_h), 0.0)
        a_ak = jnp.where(strict, _dot3_t(at_h, kt_h), 0.0)
        a_rk = jnp.where(incl, _dot3_t(rt_h, kt_h), 0.0)
        a_rb = jnp.where(incl, _dot3_t(rt_h, bt_h), 0.0)
        npow = a_ab
        tinv = eye - a_ab
        for _ in range(int(math.log2(C)) - 1):
            npow = _dot3(npow, npow)
            tinv = tinv + _dot3(tinv, npow)
        rhs = -(_dot3_t(at_h, S) + _dot3(a_ak, v_h))
        u = _dot3(tinv, rhs)
        o = _dot3_t(rt_h, S) + _dot3(a_rk, v_h) + _dot3(a_rb, u)
        s_new = S * p_tot[:, sl] + _dot3(v_h.T, kbar[:, sl]) + _dot3(u.T, bbar[:, sl])
        s_ref[h] = s_new
        y_ref[:, sl] = o


def _rwkv_chunk(r, v, kk, lw, kd, bd, B, L, heads, hd):
    T, dr = r.shape
    C = RWKV_CHUNK
    nc = L // C

    def row(d, b, c):
        return b * nc + jnp.where(d == 0, c, nc - 1 - c)

    s1 = pl.BlockSpec((C, dr), lambda d, b, c: (row(d, b, c), 0))
    s2 = pl.BlockSpec((None, C, dr), lambda d, b, c: (d, row(d, b, c), 0))
    return pl.pallas_call(
        functools.partial(_rwkv_chunk_kernel, heads=heads, hd=hd),
        grid=(2, B, nc),
        in_specs=[s1, s1, s1, s2, s2, s2],
        out_specs=s2,
        out_shape=SDS((2, T, dr), F32),
        scratch_shapes=[pltpu.VMEM((heads, hd, hd), F32)],
        compiler_params=_cparams(("parallel", "parallel", "arbitrary")), name="rwkv_chunk")(r, v, kk, lw, kd, bd)


def _rwkv_post_kernel(y_ref, r_ref, v_ref, ks_ref, g_ref, rk_ref, lnw_ref, lnb_ref, hs_ref, o_ref, *, hd):
    y = y_ref[0] + y_ref[1]
    hs = hs_ref[...]
    mean = _dot_exact_lhs(y, hs) * (1.0 / hd)
    yc = y - mean
    var = _dot_exact_lhs(yc * yc, hs) * (1.0 / hd)
    yn = yc * lax.rsqrt(var + EPS_GN) * lnw_ref[...] + lnb_ref[...]
    bonus = _dot_exact_lhs(r_ref[...] * ks_ref[...] * rk_ref[...], hs) * v_ref[...]
    o_ref[...] = (yn + bonus) * g_ref[...]


def _rwkv_post(y2, r, v, ks, g, pw, tm=512):
    T, dr = r.shape
    row = lambda i: (i, 0)
    const = lambda i: (0, 0)
    s1 = pl.BlockSpec((tm, dr), row)
    vec = pl.BlockSpec((1, dr), const)
    return pl.pallas_call(
        functools.partial(_rwkv_post_kernel, hd=pw["hd"]),
        grid=(T // tm,),
        in_specs=[pl.BlockSpec((2, tm, dr), lambda i: (0, i, 0)), s1, s1, s1, s1, vec, vec, vec,
                  pl.BlockSpec((dr, dr), const)],
        out_specs=s1,
        out_shape=SDS((T, dr), F32),
        compiler_params=_cparams(("parallel",)), name="rwkv_post")(
            y2, r, v, ks, g, pw["r_k"], pw["ln_w"], pw["ln_b"], pw["hsum"])


def _rwkv_mixer(z, B, L, pw):
    r, v, kk, g, ks, lw, kd, bd = _rwkv_pre(z, B, L, pw)
    y2 = _rwkv_chunk(r, v, kk, lw, kd, bd, B, L, pw["heads"], pw["hd"])
    return _rwkv_post(y2, r, v, ks, g, pw)


def _s5_scan_kernel(u_ref, win_ref, are_ref, aim_ref, cout_ref, y_ref, x_s, st_s, *, ns):
    d = pl.program_id(0)
    c = pl.program_id(2)
    lc, bs, cu = u_ref.shape

    @pl.when(c == 0)
    def _():
        st_s[...] = jnp.zeros_like(st_s)

    u = u_ref[...].reshape(lc * bs, cu)
    x_s[...] = _bdot(u, win_ref[...])
    are = jnp.broadcast_to(are_ref[...], (bs, ns))
    aim = jnp.broadcast_to(aim_ref[...], (bs, ns))

    def body(i, carry):
        xr, xi = carry
        t = jnp.where(d == 0, i, lc - 1 - i)
        off = pl.multiple_of(t * bs, bs)
        ir = x_s[pl.ds(off, bs), 0:ns]
        ii = x_s[pl.ds(off, bs), ns:2 * ns]
        nr = are * xr - aim * xi + ir
        ni = are * xi + aim * xr + ii
        x_s[pl.ds(off, bs), 0:ns] = nr
        x_s[pl.ds(off, bs), ns:2 * ns] = ni
        return nr, ni

    xr, xi = lax.fori_loop(0, lc, body, (st_s[0], st_s[1]))
    st_s[0] = xr
    st_s[1] = xi
    y = _bdot(x_s[...], cout_ref[...])
    y_ref[...] = y.reshape(lc, bs, y_ref.shape[-1])


def _s5_scan(u3, sw, lc=64):
    L, B, cu = u3.shape
    bs = V7X_SUBLANES
    ns = sw["ns"]
    nl = L // lc

    def tix(d, c):
        return jnp.where(d == 0, c, nl - 1 - c)

    return pl.pallas_call(
        functools.partial(_s5_scan_kernel, ns=ns),
        grid=(2, B // bs, nl),
        in_specs=[pl.BlockSpec((lc, bs, cu), lambda d, b, c: (tix(d, c), b, 0)),
                  pl.BlockSpec((None, cu, 2 * ns), lambda d, b, c: (d, 0, 0)),
                  pl.BlockSpec((None, 1, ns), lambda d, b, c: (d, 0, 0)),
                  pl.BlockSpec((None, 1, ns), lambda d, b, c: (d, 0, 0)),
                  pl.BlockSpec((2 * ns, cu), lambda d, b, c: (0, 0))],
        out_specs=pl.BlockSpec((None, lc, bs, cu), lambda d, b, c: (d, tix(d, c), b, 0)),
        out_shape=SDS((2, L, B, cu), F32),
        scratch_shapes=[pltpu.VMEM((lc * bs, 2 * ns), F32), pltpu.VMEM((2, bs, ns), F32)],
        compiler_params=_cparams(("parallel", "parallel", "arbitrary")), name="s5_scan")(
            u3, sw["win"], sw["a_re"], sw["a_im"], sw["cout"])


def _gelu_tanh(x):
    return 0.5 * x * (1.0 + jnp.tanh(math.sqrt(2.0 / math.pi) * (x + 0.044715 * (x * x * x))))


def _s5_post_kernel(y_ref, u_ref, d_ref, wg_ref, g_ref, o_ref):
    u = u_ref[...]
    y = y_ref[0] + y_ref[1] + d_ref[...] * u
    h = _gelu_tanh(y)
    out = h * _sigmoid(_bdot(h, wg_ref[...]))
    o_ref[...] = _rms(out, g_ref[...])


def _s5_post(y2, u_tm, B, L, sw, tl=512):
    cu = sw["cu"]
    nl = L // tl
    const = lambda b, l: (0, 0)
    return pl.pallas_call(
        _s5_post_kernel,
        grid=(B, nl),
        in_specs=[pl.BlockSpec((2, tl, cu), lambda b, l: (0, l, b)),
                  pl.BlockSpec((tl, cu), lambda b, l: (l, b)),
                  pl.BlockSpec((1, cu), const), pl.BlockSpec((cu, cu), const), pl.BlockSpec((1, cu), const)],
        out_specs=pl.BlockSpec((tl, cu), lambda b, l: (b * nl + l, 0)),
        out_shape=SDS((B * L, cu), F32),
        compiler_params=_cparams(("parallel", "parallel")), name="s5_post")(
            y2.reshape(2, L, B * cu), u_tm, sw["d"], sw["w_glu"], sw["norm_g"])


def _s5_mixer(u_tm, B, L, sw):
    cu = sw["cu"]
    y2 = _s5_scan(u_tm.reshape(L, B, cu), sw)
    return _s5_post(y2, u_tm, B, L, sw)


def _hy_pre_kernel(z_ref, zp_ref, zn_ref, cw_ref, cb_ref, v_o, x1_o, x2_o, *, dh):
    z = z_ref[...]
    prev_row, next_row = _halo_rows(zp_ref, zn_ref)
    zprev, znext = _shift_rows(z, prev_row, next_row)
    zc = cw_ref[0:1, :] * zprev + cw_ref[1:2, :] * z + cw_ref[2:3, :] * znext + cb_ref[...]
    v_o[...] = zc[:, 0:dh]
    x1_o[...] = zc[:, dh:2 * dh]
    x2_o[...] = zc[:, 2 * dh:3 * dh]


def _hy_pre(z, B, L, hw, tm=512):
    T, ch = z.shape
    dh = hw["dh"]
    nl = L // tm
    row = lambda b, l: (b * nl + l, 0)
    const = lambda b, l: (0, 0)
    prev, nxt = _halo_specs(tm, ch, nl, T)
    o = pl.BlockSpec((tm, dh), row)
    return pl.pallas_call(
        functools.partial(_hy_pre_kernel, dh=dh),
        grid=(B, nl),
        in_specs=[pl.BlockSpec((tm, ch), row), prev, nxt,
                  pl.BlockSpec((3, ch), const), pl.BlockSpec((1, ch), const)],
        out_specs=[o, o, o],
        out_shape=[SDS((T, dh), F32)] * 3,
        compiler_params=_cparams(("parallel", "parallel")), name="hy_pre")(z, z, z, hw["conv_w"], hw["conv_b"])


def _hy_filter_kernel(ft_ref, w1_ref, b1_ref, w2_ref, b2_ref, w3_ref, b3_ref, fr_ref, dl_ref, bm_ref,
                      h_o, s_o, *, L, tl):
    i = pl.program_id(0)
    fr = fr_ref[...]
    h = jnp.sin(fr * (_dot3(ft_ref[...], w1_ref[...]) + b1_ref[...]))
    h = jnp.sin(fr * (_dot3(h, w2_ref[...]) + b2_ref[...]))
    h = _dot3(h, w3_ref[...]) + b3_ref[...]
    pos = (i * tl + lax.broadcasted_iota(I32, (tl, 1), 0)).astype(F32)
    t = pos * (1.0 / (L - 1))
    h = h * jnp.exp(-t * dl_ref[...])
    h_o[...] = h
    keep = jnp.where((pos == 0.0) & (bm_ref[...] > 0.5), 0.0, 1.0)
    part = jnp.sum(jnp.abs(h) * keep, axis=0, keepdims=True)

    @pl.when(i == 0)
    def _():
        s_o[...] = jnp.zeros_like(s_o)

    s_o[...] += part


def _hy_filter(L, hw, tl=256):
    feats = _hyena_features(L)
    ncol = hw["f_w3"].shape[1]
    fw = hw["f_w1"].shape[1]
    const = lambda i: (0, 0)
    full = lambda a: pl.BlockSpec(a.shape, const)
    args = [hw["f_w1"], hw["f_b1"], hw["f_w2"], hw["f_b2"], hw["f_w3"], hw["f_b3"], hw["f_freq"],
            hw["abs_deltas"], hw["bwd_mask"]]
    return pl.pallas_call(
        functools.partial(_hy_filter_kernel, L=L, tl=tl),
        grid=(L // tl,),
        in_specs=[pl.BlockSpec((tl, feats.shape[1]), lambda i: (i, 0))] + [full(a) for a in args],
        out_specs=[pl.BlockSpec((tl, ncol), lambda i: (i, 0)), pl.BlockSpec((1, ncol), const)],
        out_shape=[SDS((L, ncol), F32), SDS((1, ncol), F32)],
        compiler_params=_cparams(("arbitrary",)), name="hy_filter")(feats, *args)


def _hyena_features(L):
    emb_bands = 16
    t = jnp.linspace(0.0, 1.0, L, dtype=F32)[:, None]
    w = (2.0 * math.pi / L) * jnp.arange(L, dtype=F32)[:, None]
    f = jnp.linspace(1e-4, emb_bands - 1, emb_bands, dtype=F32)[None, :]
    feats = jnp.concatenate([t, jnp.cos(f * w), -jnp.sin(f * w)], axis=-1)
    return jnp.pad(feats, ((0, 0), (0, V7X_LANES - feats.shape[1])))


def _dft_tables(N):
    n2 = DFT_INNER
    n1 = N // n2
    two_pi = 2.0 * math.pi

    def cs(num, den):
        ang = (num % den).astype(F32) * (two_pi / den)
        return jnp.cos(ang), jnp.sin(ang)

    f1 = jnp.arange(n1, dtype=I32)
    c1, s1 = cs(f1[:, None] * f1[None, :], n1)
    fwd_full = jnp.concatenate([jnp.concatenate([c1, s1], 1), jnp.concatenate([-s1, c1], 1)], 0)
    h = n1 // 2
    fwd_half = jnp.concatenate([jnp.concatenate([c1[:, :h], s1[:, :h]], 1),
                                jnp.concatenate([-s1[:, :h], c1[:, :h]], 1)], 0)
    ci, si = c1[:h, :], s1[:h, :]
    inv_half = jnp.concatenate([jnp.concatenate([ci, -si], 1), jnp.concatenate([si, ci], 1)], 0) * (1.0 / N)
    f2 = jnp.arange(n2, dtype=I32)
    freq = f1[:, None, None] + n1 * f2[None, :, None]
    c2, s2 = cs(freq * f2[None, None, :], N)
    b_fwd = jnp.concatenate([jnp.concatenate([c2, s2], 2), jnp.concatenate([-s2, c2], 2)], 1)
    b_inv = jnp.swapaxes(b_fwd, 1, 2)
    return dict(n1=n1, n2=n2, fwd_full=fwd_full, fwd_half=fwd_half, inv_half=inv_half, b_fwd=b_fwd, b_inv=b_inv)


def _lead_kernel(*refs, has_inv, has_fwd, has_scale, precise):
    refs = list(refs)
    mm = _dot3 if precise else _bdot
    if has_inv:
        yp_ref, inv_ref, xga_ref, xgb_ref, ua_ref, ub_ref, bias_ref = refs[:7]
        refs = refs[7:]
        yp = yp_ref[...]
        ystk = yp.reshape(yp.shape[0] * yp.shape[1], yp.shape[2])
        y = mm(inv_ref[...], ystk)
        k = y.shape[0] // 2
        bias = bias_ref[...]
        za = xga_ref[...] * (y[:k] + bias * ua_ref[...])
        zb = xgb_ref[...] * (y[k:] + bias * ub_ref[...])
    else:
        xa_ref, xb_ref = refs[:2]
        refs = refs[2:]
        za, zb = xa_ref[...], xb_ref[...]
        if has_scale:
            sc = refs[0][...]
            refs = refs[1:]
            za, zb = za * sc, zb * sc
    if has_fwd:
        fwd_ref = refs[0]
        refs = refs[1:]
    if has_inv:
        refs[0][...] = za
        refs[1][...] = zb
        refs = refs[2:]
    if has_fwd:
        y_o = refs[0]
        yy = mm(fwd_ref[...], jnp.concatenate([za, zb], axis=0))
        y_o[...] = yy.reshape(y_o.shape)


def _lead_call(*, pairs, n1, kin, lanes, wl, inv=None, fwd=None, xa=None, xb_off=None, scale=None,
               gate=None, precise=False, name="hy_lead"):
    nj = lanes // wl
    in_specs, args = [], []
    out_specs, out_shape = [], []
    has_inv = gate is not None
    if has_inv:
        kh = gate["xg"].shape[1]
        off = gate["off"]
        sa = pl.BlockSpec((None, kh, wl), lambda p, j: (p, 0, j))
        sb = pl.BlockSpec((None, kh, wl), lambda p, j: (p + off, 0, j))
        in_specs += [pl.BlockSpec((None, 2, n1, wl), lambda p, j: (p, 0, 0, j)),
                     pl.BlockSpec(inv.shape, lambda p, j: (0, 0)), sa, sb, sa, sb,
                     pl.BlockSpec((1, wl), lambda p, j: (0, j))]
        args += [gate["yp"], inv, gate["xg"], gate["xg"], gate["u"], gate["u"], gate["bias"]]
    else:
        in_specs += [pl.BlockSpec((None, kin, wl), lambda p, j: (p, 0, j)),
                     pl.BlockSpec((None, kin, wl), lambda p, j: (p + xb_off, 0, j))]
        args += [xa, xa]
        if scale is not None:
            in_specs.append(pl.BlockSpec((1, wl), lambda p, j: (0, j)))
            args.append(scale)
    if fwd is not None:
        in_specs.append(pl.BlockSpec(fwd.shape, lambda p, j: (0, 0)))
        args.append(fwd)
    if has_inv:
        out_specs += [sa, sa]
        out_shape += [SDS((pairs, kh, lanes), F32)] * 2
    if fwd is not None:
        out_specs.append(pl.BlockSpec((None, 2, n1, wl), lambda p, j: (p, 0, 0, j)))
        out_shape.append(SDS((pairs, 2, n1, lanes), F32))
    return pl.pallas_call(
        functools.partial(_lead_kernel, has_inv=has_inv, has_fwd=fwd is not None,
                          has_scale=scale is not None, precise=precise),
        grid=(pairs, nj), in_specs=in_specs, out_specs=out_specs, out_shape=out_shape,
        compiler_params=_cparams(("parallel", "parallel")), name=name)(*args)


def _inner_kernel(y_ref, bf_ref, *rest, f1t, n2, has_inv, precise):
    mm = _dot3 if precise else _bdot
    if has_inv:
        k_ref, bi_ref, o_ref = rest
    else:
        (o_ref,) = rest
    for i in range(f1t):
        ys = jnp.concatenate([y_ref[0, i], y_ref[1, i]], axis=0)
        x = mm(bf_ref[i], ys)
        xr, xi = x[:n2], x[n2:]
        if has_inv:
            kr, ki = k_ref[0, i], k_ref[1, i]
            zr = xr * kr - xi * ki
            zi = xr * ki + xi * kr
            yp = mm(bi_ref[i], jnp.concatenate([zr, zi], axis=0))
            o_ref[0, i] = yp[:n2]
            o_ref[1, i] = yp[n2:]
        else:
            o_ref[0, i] = xr
            o_ref[1, i] = xi


def _inner_call(y5, b_fwd, b_inv=None, kf=None, kblk=0, precise=False, f1t=8, name="hy_inner"):
    P, _, n1, n2, ch = y5.shape
    has_inv = kf is not None
    ys = pl.BlockSpec((None, 2, f1t, n2, ch), lambda j, p: (p, 0, j, 0, 0))
    ts = pl.BlockSpec((f1t, 2 * n2, 2 * n2), lambda j, p: (j, 0, 0))
    in_specs = [ys, ts]
    args = [y5, b_fwd]
    if has_inv:
        in_specs += [pl.BlockSpec((2, f1t, n2, ch), lambda j, p: (0, j, 0, kblk)), ts]
        args += [kf, b_inv]
    return pl.pallas_call(
        functools.partial(_inner_kernel, f1t=f1t, n2=n2, has_inv=has_inv, precise=precise),
        grid=(n1 // f1t, P), in_specs=in_specs, out_specs=ys, out_shape=SDS(y5.shape, F32),
        compiler_params=_cparams(("parallel", "parallel")), name=name)(*args)


def _hy_spectrum(L, hw, tabs):
    N = 2 * L
    n1, n2 = tabs["n1"], tabs["n2"]
    dh = hw["dh"]
    hwin, asum = _hy_filter(L, hw)
    ks, ss = [], []
    for o in range(2):
        kf = hwin[:, (2 * o) * dh:(2 * o + 1) * dh]
        kb = hwin[:, (2 * o + 1) * dh:(2 * o + 2) * dh]
        ks.append(jnp.concatenate([kf, jnp.zeros((1, dh), F32), kb[:0:-1]], axis=0))
        ss.append(asum[:, (2 * o) * dh:(2 * o + 1) * dh] + asum[:, (2 * o + 1) * dh:(2 * o + 2) * dh])
    k2 = jnp.concatenate(ks, axis=1)
    inv_mass = 1.0 / jnp.concatenate(ss, axis=1)
    lanes = n2 * 2 * dh
    kin = jnp.stack([k2.reshape(n1, lanes), jnp.zeros((n1, lanes), F32)])
    (y,) = _lead_call(pairs=1, n1=n1, kin=n1, lanes=lanes, wl=2048, fwd=tabs["fwd_full"], xa=kin, xb_off=1,
                      scale=jnp.tile(inv_mass, (1, n2)), precise=True, name="hy_filter_lead")
    kf = _inner_call(y.reshape(1, 2, n1, n2, 2 * dh), tabs["b_fwd"], precise=True, name="hy_filter_inner")
    return kf[0]


def _hyena_mixer(z, B, L, hw, tabs, kf):
    dh = hw["dh"]
    n1, n2 = tabs["n1"], tabs["n2"]
    kh = n1 // 2
    lanes = n2 * dh
    P = B // 2
    wl = 2048
    v, x1, x2 = _hy_pre(z, B, L, hw)
    view = lambda a: a.reshape(B, kh, lanes)
    v3, x13, x23 = view(v), view(x1), view(x2)
    bias = [jnp.tile(hw["bias"][o:o + 1], (1, n2)) for o in range(2)]
    (y,) = _lead_call(pairs=P, n1=n1, kin=kh, lanes=lanes, wl=wl, fwd=tabs["fwd_half"], xa=v3, xb_off=P,
                      name="hy_lead0")
    yp = _inner_call(y.reshape(P, 2, n1, n2, dh), tabs["b_fwd"], tabs["b_inv"], kf, kblk=0, name="hy_inner0")
    z1a, z1b, y = _lead_call(pairs=P, n1=n1, kin=kh, lanes=lanes, wl=wl, inv=tabs["inv_half"], fwd=tabs["fwd_half"],
                             gate=dict(yp=yp.reshape(P, 2, n1, lanes), xg=x13, u=v3, bias=bias[0], off=P),
                             name="hy_lead1")
    z1 = jnp.concatenate([z1a, z1b], axis=0)
    yp = _inner_call(y.reshape(P, 2, n1, n2, dh), tabs["b_fwd"], tabs["b_inv"], kf, kblk=1, name="hy_inner1")
    ya, yb = _lead_call(pairs=P, n1=n1, kin=kh, lanes=lanes, wl=wl, inv=tabs["inv_half"],
                        gate=dict(yp=yp.reshape(P, 2, n1, lanes), xg=x23, u=z1, bias=bias[1], off=P),
                        name="hy_lead2")
    return jnp.concatenate([ya, yb], axis=0).reshape(B * L, dh)


def _outproj_kernel(yr_ref, ys_ref, yh_ref, x_ref, wo_ref, gh_ref, gf_ref, wr_ref, xo_ref, xn_ref, aff_ref,
                    *, dr, ds):
    yh = _rms(yh_ref[...], gh_ref[...])
    acc = _bdot(yr_ref[...], wo_ref[0:dr, :])
    acc += _bdot(ys_ref[...], wo_ref[dr:dr + ds, :])
    acc += _bdot(yh, wo_ref[dr + ds:, :])
    xnew = x_ref[...] + acc
    xo_ref[...] = xnew
    xn = _rms(xnew, gf_ref[...])
    xn_ref[...] = xn
    logits = _dot3_t(wr_ref[...], xn)
    m = jnp.max(logits, axis=0, keepdims=True)
    e = jnp.exp(logits - m)
    aff_ref[...] = e / jnp.sum(e, axis=0, keepdims=True)


def _out_proj(yr, ys, yh, x, lw, tm=256):
    T, D = x.shape
    dr, ds, dh = yr.shape[1], ys.shape[1], yh.shape[1]
    E = lw["w_router_t"].shape[0]
    row = lambda i: (i, 0)
    const = lambda i: (0, 0)
    return pl.pallas_call(
        functools.partial(_outproj_kernel, dr=dr, ds=ds),
        grid=(T // tm,),
        in_specs=[pl.BlockSpec((tm, dr), row), pl.BlockSpec((tm, ds), row), pl.BlockSpec((tm, dh), row),
                  pl.BlockSpec((tm, D), row), pl.BlockSpec((D, D), const), pl.BlockSpec((1, dh), const),
                  pl.BlockSpec((1, D), const), pl.BlockSpec((E, D), const)],
        out_specs=[pl.BlockSpec((tm, D), row), pl.BlockSpec((tm, D), row), pl.BlockSpec((E, tm), lambda i: (0, i))],
        out_shape=[SDS((T, D), F32), SDS((T, D), F32), SDS((E, T), F32)],
        compiler_params=_cparams(("parallel",)), name="out_proj")(
            yr, ys, yh, x, lw["w_out"], lw["hy"]["norm_g"], lw["norm_ffn_g"], lw["w_router_t"])


def _select_prefix_kernel(aff_ref, incl_ref, off_ref, *, cap):
    aff = aff_ref[...]
    nt = aff.shape[0]
    bits = pltpu.bitcast(aff, I32)

    def body(i, thr):
        cand = thr | jnp.left_shift(jnp.int32(1), 30 - i)
        cnt = jnp.sum(jnp.where(bits >= cand, 1.0, 0.0))
        return jnp.where(cnt >= cap, cand, thr)

    thr = lax.fori_loop(0, 31, body, jnp.int32(0))
    gt = bits > thr
    eq = bits == thr
    need = cap - jnp.sum(jnp.where(gt, 1.0, 0.0))

    li = lax.broadcasted_iota(I32, (V7X_LANES, V7X_LANES), 0)
    mi = lax.broadcasted_iota(I32, (V7X_LANES, V7X_LANES), 1)
    upper = jnp.where(li <= mi, 1.0, 0.0).astype(BF16)
    ones = jnp.ones((V7X_LANES, V7X_LANES), BF16)
    ri = lax.broadcasted_iota(I32, (nt, nt), 0)
    ci = lax.broadcasted_iota(I32, (nt, nt), 1)
    lower = jnp.where(ci < ri, 1.0, 0.0).astype(BF16)
    dot = functools.partial(jnp.dot, preferred_element_type=F32)

    def prefix(m):
        mb = m.astype(BF16)
        incl = dot(mb, upper)
        tot = dot(mb, ones)
        return incl, dot(lower, tot.astype(BF16))

    eqf = jnp.where(eq, 1.0, 0.0)
    incl_eq, off_eq = prefix(eqf)
    rank_eq = off_eq + incl_eq - eqf
    sel = jnp.where(gt, 1.0, jnp.where(eq & (rank_eq < need), 1.0, 0.0))
    incl, off = prefix(sel)
    incl_ref[...] = incl
    off_ref[...] = off


def _select_index_kernel(incl_ref, off_ref, aff_ref, idx_ref, gate_ref, *, sb):
    j0 = pl.program_id(1) * sb
    incl = incl_ref[...]
    off = off_ref[...]
    nt = incl.shape[0]
    off_col = off[:, 0:1]
    end_col = off_col + incl[:, V7X_LANES - 1:V7X_LANES]
    j = (j0 + lax.broadcasted_iota(I32, (1, sb), 1)).astype(F32)
    tile_id = jnp.sum(jnp.where(end_col <= j, 1.0, 0.0), axis=0, keepdims=True)
    ti = lax.broadcasted_iota(I32, (nt, sb), 0).astype(F32)
    onehot = jnp.where(ti == tile_id, 1.0, 0.0)
    excl = jnp.sum(onehot * off_col, axis=0, keepdims=True)
    rj = j - excl
    ohb = onehot.astype(BF16)
    dot = functools.partial(jnp.dot, preferred_element_type=F32)
    rows = dot(incl.T.astype(BF16), ohb)
    local = jnp.sum(jnp.where(rows <= rj, 1.0, 0.0), axis=0, keepdims=True)
    idx_ref[...] = (tile_id * V7X_LANES + local).astype(I32)
    ah, am, al = _split3(aff_ref[...].T)
    arows = dot(ah, ohb) + dot(am, ohb) + dot(al, ohb)
    lane = lax.broadcasted_iota(I32, (V7X_LANES, sb), 0).astype(F32)
    gate_ref[...] = jnp.sum(jnp.where(lane == local, arows, 0.0), axis=0, keepdims=True)


def _moe_select(aff_t, cap, sb=1024):
    E, T = aff_t.shape
    nt = T // V7X_LANES
    aff3 = aff_t.reshape(E, nt, V7X_LANES)
    s3 = pl.BlockSpec((None, nt, V7X_LANES), lambda e: (e, 0, 0))
    incl, off = pl.pallas_call(
        functools.partial(_select_prefix_kernel, cap=cap),
        grid=(E,), in_specs=[s3], out_specs=[s3, s3],
        out_shape=[SDS((E, nt, V7X_LANES), F32)] * 2,
        compiler_params=_cparams(("parallel",)), name="moe_select_prefix")(aff3)
    s3b = pl.BlockSpec((None, nt, V7X_LANES), lambda e, j: (e, 0, 0))
    so = pl.BlockSpec((None, 1, sb), lambda e, j: (e, 0, j))
    idx, gate = pl.pallas_call(
        functools.partial(_select_index_kernel, sb=sb),
        grid=(E, cap // sb), in_specs=[s3b, s3b, s3b], out_specs=[so, so],
        out_shape=[SDS((E, 1, cap), I32), SDS((E, 1, cap), F32)],
        compiler_params=_cparams(("parallel", "parallel")), name="moe_select_index")(incl, off, aff3)
    return idx, gate


def _moe_ffn_kernel(idx_hbm, gate_ref, xn_hbm, acc_in_hbm, wg_ref, wu_ref, wd_ref, acc_hbm,
                    idx_s, xbuf, abuf, sem_i, sem_x, sem_a, sem_s, *, m, nt):
    del acc_in_hbm
    e = pl.program_id(0)
    i = pl.program_id(1)
    cp = pltpu.make_async_copy(idx_hbm.at[e * nt + i], idx_s, sem_i)
    cp.start()
    cp.wait()

    def issue(r, carry):
        t = idx_s[r]
        pltpu.make_async_copy(xn_hbm.at[t], xbuf.at[r], sem_x).start()
        pltpu.make_async_copy(acc_hbm.at[t], abuf.at[r], sem_a).start()
        return carry

    lax.fori_loop(0, m, issue, 0)
    pltpu.make_async_copy(xn_hbm.at[pl.ds(0, m)], xbuf, sem_x).wait()
    pltpu.make_async_copy(acc_hbm.at[pl.ds(0, m)], abuf, sem_a).wait()

    x = xbuf[...].astype(BF16)
    hg = jnp.dot(x, wg_ref[...], preferred_element_type=F32)
    hu = jnp.dot(x, wu_ref[...], preferred_element_type=F32)
    h = hg * _sigmoid(hg) * hu
    y = jnp.dot(h.astype(BF16), wd_ref[...], preferred_element_type=F32)
    g_col = jnp.broadcast_to(gate_ref[...], (V7X_LANES, m)).T[:, 0:1]
    abuf[...] = abuf[...] + y * g_col

    def scatter(r, carry):
        t = idx_s[r]
        pltpu.make_async_copy(abuf.at[r], acc_hbm.at[t], sem_s).start()
        return carry

    lax.fori_loop(0, m, scatter, 0)
    pltpu.make_async_copy(abuf, acc_hbm.at[pl.ds(0, m)], sem_s).wait()


def _moe_ffn(idx, gate, xn, acc, mw, m=512):
    T, D = xn.shape
    E, _, cap = idx.shape
    nt = cap // m
    F = mw["w_gate"].shape[2]
    idx2 = idx.reshape(E * nt, m)
    gate3 = gate.reshape(E * nt, 1, m)
    wspec = lambda a, b: pl.BlockSpec((None, a, b), lambda e, i: (e, 0, 0))
    any_spec = pl.BlockSpec(memory_space=pl.ANY)
    return pl.pallas_call(
        functools.partial(_moe_ffn_kernel, m=m, nt=nt),
        grid=(E, nt),
        in_specs=[any_spec, pl.BlockSpec((None, 1, m), lambda e, i: (e * nt + i, 0, 0)), any_spec, any_spec,
                  wspec(D, F), wspec(D, F), wspec(F, D)],
        out_specs=any_spec,
        out_shape=SDS((T, D), F32),
        input_output_aliases={3: 0},
        scratch_shapes=[pltpu.SMEM((m,), I32), pltpu.VMEM((m, D), F32), pltpu.VMEM((m, D), F32),
                        pltpu.SemaphoreType.DMA, pltpu.SemaphoreType.DMA, pltpu.SemaphoreType.DMA,
                        pltpu.SemaphoreType.DMA],
        compiler_params=_cparams(("arbitrary", "arbitrary")), name="moe_ffn")(
            idx2, gate3, xn, acc, mw["w_gate"], mw["w_up"], mw["w_down"])


def _moe(xnew, xn, aff_t, mw):
    T = xn.shape[0]
    E = aff_t.shape[0]
    cap = EC_CAPACITY_FACTOR * T // E
    idx, gate = _moe_select(aff_t, cap)
    return _moe_ffn(idx, gate, xn, xnew, mw)


def _norm_kernel(x_ref, g_ref, o_ref):
    o_ref[...] = _rms(x_ref[...], g_ref[...])


def _final_norm(x, g, tm=512):
    T, D = x.shape
    return pl.pallas_call(
        _norm_kernel, grid=(T // tm,),
        in_specs=[pl.BlockSpec((tm, D), lambda i: (i, 0)), pl.BlockSpec((1, D), lambda i: (0, 0))],
        out_specs=pl.BlockSpec((tm, D), lambda i: (i, 0)), out_shape=SDS((T, D), F32),
        compiler_params=_cparams(("parallel",)), name="final_norm")(x, g)


def _block_diag_rows(blocks):
    n, r, c = blocks.shape
    eye = jnp.eye(n, dtype=blocks.dtype)
    return jnp.einsum("nrc,nm->nrmc", blocks, eye).reshape(n * r, n * c)


def _rwkv_weights(i, p):
    heads, hd = p["rwkv_r_k"].shape[1:]
    dr = heads * hd
    row = lambda a: a.reshape(1, -1)
    head_id = jnp.arange(dr) // hd
    return dict(
        dr=dr, heads=heads, hd=hd,
        mu_prev=row(p["rwkv_mu_prev"][i]), mu_next=row(p["rwkv_mu_next"][i]),
        w0=row(p["rwkv_w0"][i]), wup=_block_diag_rows(p["rwkv_w_up"][i]).astype(BF16),
        a0=row(p["rwkv_a0"][i]), aup=_block_diag_rows(p["rwkv_a_up"][i]).astype(BF16),
        gup=p["rwkv_g_up"][i].astype(BF16),
        k_k=row(p["rwkv_k_k"][i]), k_a=row(p["rwkv_k_a"][i]), r_k=row(p["rwkv_r_k"][i]),
        ln_w=row(p["rwkv_ln_w"][i]), ln_b=row(p["rwkv_ln_b"][i]),
        hsum=(head_id[:, None] == head_id[None, :]).astype(BF16))


def _s5_weights(i, p):
    lam_re, lam_im, log_dt = p["s5_lam_re"][i], p["s5_lam_im"][i], p["s5_log_dt"][i]
    b_re, b_im, c_re, c_im = p["s5_b_re"][i], p["s5_b_im"][i], p["s5_c_re"][i], p["s5_c_im"][i]
    G, P, H = b_re.shape
    l_re = jnp.minimum(lam_re, -1e-4)
    dt = jnp.exp(log_dt)[..., None]
    mag = jnp.exp(l_re * dt)
    ab_re = mag * jnp.cos(lam_im * dt)
    ab_im = mag * jnp.sin(lam_im * dt)
    den = l_re * l_re + lam_im * lam_im
    n_re = ab_re - 1.0
    f_re = (n_re * l_re + ab_im * lam_im) / den
    f_im = (ab_im * l_re - n_re * lam_im) / den
    w_re = jnp.swapaxes(f_re[..., None] * b_re - f_im[..., None] * b_im, 2, 3)
    w_im = jnp.swapaxes(f_re[..., None] * b_im + f_im[..., None] * b_re, 2, 3)
    win = jnp.stack([jnp.concatenate([_block_diag_rows(w_re[d]), _block_diag_rows(w_im[d])], axis=1)
                     for d in range(2)]).astype(BF16)
    cout = jnp.concatenate([_block_diag_rows(jnp.swapaxes(c_re, 1, 2)),
                            _block_diag_rows(jnp.swapaxes(-c_im, 1, 2))], axis=0).astype(BF16)
    ns = G * P
    return dict(ns=ns, cu=G * H, win=win, cout=cout,
                a_re=ab_re.reshape(2, 1, ns), a_im=ab_im.reshape(2, 1, ns),
                d=p["s5_d"][i].reshape(1, -1), w_glu=p["s5_w_glu"][i].astype(BF16),
                norm_g=p["s5_norm_g"][i].reshape(1, -1))


def _hyena_weights(i, p):
    dh = p["hy_norm_g"].shape[1]
    emb = p["hy_f_w1"].shape[1]
    w1 = jnp.pad(p["hy_f_w1"][i], ((0, V7X_LANES - emb), (0, 0)))
    deltas = jnp.linspace(math.log(HYENA_TARGET) / HYENA_FAST_DECAY, math.log(HYENA_TARGET) / HYENA_SLOW_DECAY,
                          dh, dtype=F32)
    ncol = p["hy_f_w3"].shape[2]
    col_dir = (jnp.arange(ncol) // dh) % 2
    row = lambda a: a.reshape(1, -1)
    return dict(dh=dh, conv_w=p["hy_conv_w"][i], conv_b=row(p["hy_conv_b"][i]),
                f_w1=w1, f_b1=row(p["hy_f_b1"][i]), f_w2=p["hy_f_w2"][i], f_b2=row(p["hy_f_b2"][i]),
                f_w3=p["hy_f_w3"][i], f_b3=row(p["hy_f_b3"][i]), f_freq=row(p["hy_f_freq"][i]),
                abs_deltas=jnp.tile(jnp.abs(deltas), ncol // dh).reshape(1, -1),
                bwd_mask=col_dir.astype(F32).reshape(1, -1),
                bias=p["hy_bias"][i], norm_g=row(p["hy_norm_g"][i]))


def _layer_weights(i, p):
    return dict(
        norm_mix_g=p["norm_mix_g"][i].reshape(1, -1), w_in=p["w_in"][i].astype(BF16),
        rwkv=_rwkv_weights(i, p), s5=_s5_weights(i, p), hy=_hyena_weights(i, p),
        w_out=p["w_out"][i].astype(BF16), norm_ffn_g=p["norm_ffn_g"][i].reshape(1, -1),
        w_router_t=p["moe_w_router"][i].T,
        moe=dict(w_gate=p["moe_w_gate"][i].astype(BF16), w_up=p["moe_w_up"][i].astype(BF16),
                 w_down=p["moe_w_down"][i].astype(BF16)))


def _layer(x, B, L, lw, tabs, kf):
    c_r = lw["rwkv"]["mu_prev"].shape[1]
    c_s = lw["s5"]["cu"]
    c_h = 3 * lw["hy"]["dh"]
    z_r, z_s, z_h = _in_proj(x, B, L, lw["norm_mix_g"], lw["w_in"], c_r, c_s, c_h)
    y_r = _rwkv_mixer(z_r, B, L, lw["rwkv"])
    y_s = _s5_mixer(z_s, B, L, lw["s5"])
    y_h = _hyena_mixer(z_h, B, L, lw["hy"], tabs, kf)
    xnew, xn, aff_t = _out_proj(y_r, y_s, y_h, x, lw)
    return _moe(xnew, xn, aff_t, lw["moe"])


def _trunk(x, layers, final_g):
    B, L, D = x.shape
    tabs = _dft_tables(2 * L)
    h = x.reshape(B * L, D)
    for lw in layers:
        kf = _hy_spectrum(L, lw["hy"], tabs)
        h = _layer(h, B, L, lw, tabs, kf)
    return _final_norm(h, final_g.reshape(1, -1)).reshape(B, L, D)


def kernel(x_prompt, x_sample, norm_mix_g, w_in, rwkv_mu_prev, rwkv_mu_next, rwkv_w0, rwkv_w_up, rwkv_a0, rwkv_a_up, rwkv_g_up, rwkv_k_k, rwkv_k_a, rwkv_r_k, rwkv_ln_w, rwkv_ln_b, s5_lam_re, s5_lam_im, s5_log_dt, s5_b_re, s5_b_im, s5_c_re, s5_c_im, s5_d, s5_w_glu, s5_norm_g, hy_conv_w, hy_conv_b, hy_f_w1, hy_f_b1, hy_f_w2, hy_f_b2, hy_f_w3, hy_f_b3, hy_f_freq, hy_bias, hy_norm_g, w_out, norm_ffn_g, moe_w_router, moe_w_gate, moe_w_up, moe_w_down, final_norm_g):
    p = dict(norm_mix_g=norm_mix_g, w_in=w_in, rwkv_mu_prev=rwkv_mu_prev, rwkv_mu_next=rwkv_mu_next,
             rwkv_w0=rwkv_w0, rwkv_w_up=rwkv_w_up, rwkv_a0=rwkv_a0, rwkv_a_up=rwkv_a_up,
             rwkv_g_up=rwkv_g_up, rwkv_k_k=rwkv_k_k, rwkv_k_a=rwkv_k_a, rwkv_r_k=rwkv_r_k,
             rwkv_ln_w=rwkv_ln_w, rwkv_ln_b=rwkv_ln_b, s5_lam_re=s5_lam_re, s5_lam_im=s5_lam_im,
             s5_log_dt=s5_log_dt, s5_b_re=s5_b_re, s5_b_im=s5_b_im, s5_c_re=s5_c_re, s5_c_im=s5_c_im,
             s5_d=s5_d, s5_w_glu=s5_w_glu, s5_norm_g=s5_norm_g, hy_conv_w=hy_conv_w, hy_conv_b=hy_conv_b,
             hy_f_w1=hy_f_w1, hy_f_b1=hy_f_b1, hy_f_w2=hy_f_w2, hy_f_b2=hy_f_b2, hy_f_w3=hy_f_w3,
             hy_f_b3=hy_f_b3, hy_f_freq=hy_f_freq, hy_bias=hy_bias, hy_norm_g=hy_norm_g, w_out=w_out,
             norm_ffn_g=norm_ffn_g, moe_w_router=moe_w_router, moe_w_gate=moe_w_gate,
             moe_w_up=moe_w_up, moe_w_down=moe_w_down)
    layers = [_layer_weights(i, p) for i in range(w_in.shape[0])]
    return (_trunk(x_prompt, layers, final_norm_g), _trunk(x_sample, layers, final_norm_g))
```

```python
import functools
import math

import jax
import jax.numpy as jnp
from jax import lax
from jax.experimental import pallas as pl
from jax.experimental.pallas import tpu as pltpu

F32 = jnp.float32
BF16 = jnp.bfloat16
I32 = jnp.int32
SDS = jax.ShapeDtypeStruct

EPS_NORM = 1e-6
EPS_GN = 64e-5
HYENA_FAST_DECAY = 0.3
HYENA_SLOW_DECAY = 1.5
HYENA_TARGET = 1e-2
EC_CAPACITY_FACTOR = 2

V7X_LANES = 128
V7X_SUBLANES = 8
V7X_VMEM_LIMIT_BYTES = 56 * 1024 * 1024

RWKV_CHUNK = 64
DFT_INNER = 64


def _cparams(sem, vmem=V7X_VMEM_LIMIT_BYTES):
    return pltpu.CompilerParams(dimension_semantics=sem, vmem_limit_bytes=vmem)


def _bdot(a, b):
    return jnp.dot(a.astype(BF16), b.astype(BF16), preferred_element_type=F32)


def _bdot_t(a, b):
    return lax.dot_general(a.astype(BF16), b.astype(BF16), (((1,), (1,)), ((), ())),
                           preferred_element_type=F32)


def _split2(x):
    hi = x.astype(BF16)
    lo = (x - hi.astype(F32)).astype(BF16)
    return hi, lo


def _split3(x):
    hi = x.astype(BF16)
    r1 = x - hi.astype(F32)
    mid = r1.astype(BF16)
    lo = (r1 - mid.astype(F32)).astype(BF16)
    return hi, mid, lo


def _dot3(a, b):
    ah, al = _split2(a)
    bh, bl = _split2(b)
    d = functools.partial(jnp.dot, preferred_element_type=F32)
    return d(ah, bh) + d(ah, bl) + d(al, bh)


def _dot3_t(a, b):
    ah, al = _split2(a)
    bh, bl = _split2(b)
    d = lambda x, y: lax.dot_general(x, y, (((1,), (1,)), ((), ())), preferred_element_type=F32)
    return d(ah, bh) + d(ah, bl) + d(al, bh)


def _dot_exact_rhs(m01, x):
    xh, xl = _split2(x)
    d = functools.partial(jnp.dot, preferred_element_type=F32)
    return d(m01, xh) + d(m01, xl)


def _dot_exact_lhs(x, m01):
    xh, xl = _split2(x)
    d = functools.partial(jnp.dot, preferred_element_type=F32)
    return d(xh, m01) + d(xl, m01)


def _rms(x, g):
    return x * lax.rsqrt(jnp.mean(x * x, axis=-1, keepdims=True) + EPS_NORM) * g


def _sigmoid(x):
    return 1.0 / (1.0 + jnp.exp(-x))


def _softplus(x):
    return jnp.maximum(x, 0.0) + jnp.log(1.0 + jnp.exp(-jnp.abs(x)))


def _shift_rows(z, prev_row, next_row):
    n = z.shape[0]
    rows = lax.broadcasted_iota(I32, z.shape, 0)
    zprev = jnp.where(rows == 0, prev_row, pltpu.roll(z, 1, 0))
    znext = jnp.where(rows == n - 1, next_row, pltpu.roll(z, n - 1, 0))
    return zprev, znext


def _halo_specs(tm, cols, nl, total_rows):
    r8 = tm // V7X_SUBLANES
    last = total_rows // V7X_SUBLANES - 1
    prev = pl.BlockSpec((V7X_SUBLANES, cols), lambda b, l: (jnp.maximum((b * nl + l) * r8 - 1, 0), 0))
    nxt = pl.BlockSpec((V7X_SUBLANES, cols), lambda b, l: (jnp.minimum((b * nl + l + 1) * r8, last), 0))
    return prev, nxt


def _halo_rows(zp_ref, zn_ref):
    l = pl.program_id(1)
    nl = pl.num_programs(1)
    prev_row = jnp.where(l == 0, 0.0, zp_ref[V7X_SUBLANES - 1:V7X_SUBLANES, :])
    next_row = jnp.where(l == nl - 1, 0.0, zn_ref[0:1, :])
    return prev_row, next_row


def _inproj_kernel(x_ref, g_ref, w_ref, zr_ref, zs_ref, zh_ref, *, c_r, c_s):
    h = _rms(x_ref[...], g_ref[...])
    z = _bdot(h, w_ref[...])
    zr_ref[...] = z[:, :c_r]
    zs_ref[...] = z[:, c_r:c_r + c_s]
    zh_ref[...] = z[:, c_r + c_s:]


def _in_proj(x, B, L, g, w, c_r, c_s, c_h, tm=512):
    T, D = x.shape
    nl = L // tm
    return pl.pallas_call(
        functools.partial(_inproj_kernel, c_r=c_r, c_s=c_s),
        grid=(B, nl),
        in_specs=[pl.BlockSpec((tm, D), lambda b, l: (b * nl + l, 0)),
                  pl.BlockSpec((1, D), lambda b, l: (0, 0)),
                  pl.BlockSpec((D, c_r + c_s + c_h), lambda b, l: (0, 0))],
        out_specs=[pl.BlockSpec((tm, c_r), lambda b, l: (b * nl + l, 0)),
                   pl.BlockSpec((tm, c_s), lambda b, l: (l, b)),
                   pl.BlockSpec((tm, c_h), lambda b, l: (b * nl + l, 0))],
        out_shape=[SDS((T, c_r), F32), SDS((L, B * c_s), F32), SDS((T, c_h), F32)],
        compiler_params=_cparams(("parallel", "parallel")), name="in_proj")(x, g, w)


def _rwkv_pre_kernel(z_ref, zp_ref, zn_ref, mup_ref, mun_ref, w0_ref, wup_ref, a0_ref, aup_ref, gup_ref,
                     kk_ref, ka_ref, hs_ref,
                     r_o, v_o, kkn_o, g_o, ks_o, lw_o, kd_o, bd_o, *, dr):
    z = z_ref[...]
    prev_row, next_row = _halo_rows(zp_ref, zn_ref)
    zprev, znext = _shift_rows(z, prev_row, next_row)
    zs = z + mup_ref[...] * (zprev - z) + mun_ref[...] * (znext - z)
    r = zs[:, 0:dr]
    k = zs[:, dr:2 * dr]
    v = zs[:, 2 * dr:3 * dr]
    c3 = 3 * dr
    wd = zs[:, c3:c3 + 128]
    ad = zs[:, c3 + 128:c3 + 256]
    gd = zs[:, c3 + 256:c3 + 384]
    w_raw = w0_ref[...] + _bdot(jnp.tanh(wd), wup_ref[...])
    lw = -jnp.exp(-_softplus(-w_raw) - 0.5)
    a = _sigmoid(a0_ref[...] + _bdot(ad, aup_ref[...]))
    g = _bdot(_sigmoid(gd), gup_ref[...])
    kk = k * kk_ref[...]
    ss = _dot_exact_lhs(kk * kk, hs_ref[...])
    kk = kk / jnp.maximum(jnp.sqrt(ss), 1e-12)
    ka = ka_ref[...]
    r_o[...] = r
    v_o[...] = v
    kkn_o[...] = kk
    g_o[...] = g
    ksum = None
    for d in range(2):
        a_d = a[:, d * dr:(d + 1) * dr]
        k_d = k * (1.0 + (a_d - 1.0) * ka)
        lw_o[d] = lw[:, d * dr:(d + 1) * dr]
        kd_o[d] = k_d
        bd_o[d] = kk * a_d
        ksum = k_d if ksum is None else ksum + k_d
    ks_o[...] = ksum


def _rwkv_pre(z, B, L, pw, tm=256):
    T, cr = z.shape
    dr = pw["dr"]
    nl = L // tm
    row = lambda b, l: (b * nl + l, 0)
    const = lambda b, l: (0, 0)
    prev, nxt = _halo_specs(tm, cr, nl, T)
    full = lambda a: pl.BlockSpec(a.shape, const)
    params = [pw["mu_prev"], pw["mu_next"], pw["w0"], pw["wup"], pw["a0"], pw["aup"], pw["gup"],
              pw["k_k"], pw["k_a"], pw["hsum"]]
    o1 = pl.BlockSpec((tm, dr), row)
    o2 = pl.BlockSpec((2, tm, dr), lambda b, l: (0, b * nl + l, 0))
    return pl.pallas_call(
        functools.partial(_rwkv_pre_kernel, dr=dr),
        grid=(B, nl),
        in_specs=[pl.BlockSpec((tm, cr), row), prev, nxt] + [full(a) for a in params],
        out_specs=[o1] * 5 + [o2] * 3,
        out_shape=[SDS((T, dr), F32)] * 5 + [SDS((2, T, dr), F32)] * 3,
        compiler_params=_cparams(("parallel", "parallel")), name="rwkv_pre")(z, z, z, *params)


def _bmm(a, b):
    return lax.dot_general(a.astype(BF16), b.astype(BF16), (((2,), (1,)), ((0,), (0,))),
                           preferred_element_type=F32)


def _bmm_t(a, b):
    return lax.dot_general(a.astype(BF16), b.astype(BF16), (((2,), (2,)), ((0,), (0,))),
                           preferred_element_type=F32)


def _rwkv_chunk_operands(d, r_ref, v_ref, kk_ref, lw_ref, kd_ref, bd_ref, *, heads, hd):
    C = r_ref.shape[0]
    ti = lax.broadcasted_iota(I32, (C, C), 0)
    si = lax.broadcasted_iota(I32, (C, C), 1)
    dd = (si - ti) if d == 0 else (ti - si)
    lw = lw_ref[...]
    cum = _dot_exact_rhs(jnp.where(dd <= 0, 1.0, 0.0).astype(BF16), lw)
    tot = jnp.sum(lw, axis=0, keepdims=True)
    kd = kd_ref[...]
    bd = bd_ref[...]
    p_inv = jnp.exp(-cum)
    p_end = jnp.exp(tot - cum)

    def split(x):
        return jnp.stack([x[:, h * hd:(h + 1) * hd] for h in range(heads)])

    return dict(
        incl=(dd <= 0)[None], strict=(dd < 0)[None], eye=jnp.where(dd == 0, 1.0, 0.0)[None],
        xq=split(jnp.concatenate([kk_ref[...] * jnp.exp(cum - lw), r_ref[...] * jnp.exp(cum)], axis=0).astype(BF16)),
        kt=split((kd * p_inv).astype(BF16)), bt=split((bd * p_inv).astype(BF16)),
        kb=split(jnp.concatenate([kd * p_end, bd * p_end], axis=0).astype(BF16)),
        v=split(v_ref[...].astype(BF16)), p_tot=split(jnp.exp(tot)))


def _rwkv_chunk_kernel(r0, v0, kk0, lw0, kd0, bd0, r1, v1, kk1, lw1, kd1, bd1, y0, y1, s_ref, *, heads, hd):
    @pl.when(pl.program_id(1) == 0)
    def _():
        s_ref[...] = jnp.zeros_like(s_ref)

    C = r0.shape[0]
    ops = [_rwkv_chunk_operands(0, r0, v0, kk0, lw0, kd0, bd0, heads=heads, hd=hd),
           _rwkv_chunk_operands(1, r1, v1, kk1, lw1, kd1, bd1, heads=heads, hd=hd)]
    dirs = (0, 1)
    a_k = [_bmm_t(ops[d]["xq"], ops[d]["kt"]) for d in dirs]
    a_b = [_bmm_t(ops[d]["xq"], ops[d]["bt"]) for d in dirs]
    a_kq = [jnp.concatenate([jnp.where(ops[d]["strict"], a_k[d][:, :C], 0.0),
                             jnp.where(ops[d]["incl"], a_k[d][:, C:], 0.0)], axis=1) for d in dirs]
    a_rb = [jnp.where(ops[d]["incl"], a_b[d][:, C:], 0.0) for d in dirs]
    npow = [jnp.where(ops[d]["strict"], a_b[d][:, :C], 0.0) for d in dirs]
    tinv = [ops[d]["eye"] - npow[d] for d in dirs]
    for _ in range(int(math.log2(C)) - 1):
        npow = [_bmm(npow[d], npow[d]) for d in dirs]
        tinv = [tinv[d] + _bmm(tinv[d], npow[d]) for d in dirs]
    s_old = [s_ref[d] for d in dirs]
    xs = [_bmm_t(ops[d]["xq"], s_old[d]) for d in dirs]
    av = [_bmm(a_kq[d], ops[d]["v"]) for d in dirs]
    u = [_bmm(tinv[d], -(xs[d][:, :C] + av[d][:, :C])) for d in dirs]
    o = [xs[d][:, C:] + av[d][:, C:] + _bmm(a_rb[d], u[d]) for d in dirs]
    for d, y_ref in zip(dirs, (y0, y1)):
        vu = jnp.concatenate([ops[d]["v"].astype(F32), u[d]], axis=1)
        upd = lax.dot_general(vu.astype(BF16), ops[d]["kb"], (((1,), (1,)), ((0,), (0,))),
                              preferred_element_type=F32)
        s_ref[d] = s_old[d] * ops[d]["p_tot"] + upd
        for h in range(heads):
            y_ref[:, h * hd:(h + 1) * hd] = o[d][h]


def _rwkv_chunk(r, v, kk, lw, kd, bd, B, L, heads, hd):
    T, dr = r.shape
    C = RWKV_CHUNK
    nc = L // C
    f1 = pl.BlockSpec((C, dr), lambda b, c: (b * nc + c, 0))
    b1 = pl.BlockSpec((C, dr), lambda b, c: (b * nc + nc - 1 - c, 0))
    f2 = pl.BlockSpec((None, C, dr), lambda b, c: (0, b * nc + c, 0))
    b2 = pl.BlockSpec((None, C, dr), lambda b, c: (1, b * nc + nc - 1 - c, 0))
    return pl.pallas_call(
        functools.partial(_rwkv_chunk_kernel, heads=heads, hd=hd),
        grid=(B, nc),
        in_specs=[f1, f1, f1, f2, f2, f2, b1, b1, b1, b2, b2, b2],
        out_specs=[f1, b1],
        out_shape=[SDS((T, dr), F32)] * 2,
        scratch_shapes=[pltpu.VMEM((2, heads, hd, hd), F32)],
        compiler_params=_cparams(("parallel", "arbitrary")), name="rwkv_chunk")(
            r, v, kk, lw, kd, bd, r, v, kk, lw, kd, bd)


def _rwkv_post_kernel(y0_ref, y1_ref, r_ref, v_ref, ks_ref, g_ref, rk_ref, lnw_ref, lnb_ref, hs_ref, o_ref, *, hd):
    y = y0_ref[...] + y1_ref[...]
    hs = hs_ref[...]
    mean = _dot_exact_lhs(y, hs) * (1.0 / hd)
    yc = y - mean
    var = _dot_exact_lhs(yc * yc, hs) * (1.0 / hd)
    yn = yc * lax.rsqrt(var + EPS_GN) * lnw_ref[...] + lnb_ref[...]
    bonus = _dot_exact_lhs(r_ref[...] * ks_ref[...] * rk_ref[...], hs) * v_ref[...]
    o_ref[...] = (yn + bonus) * g_ref[...]


def _rwkv_post(y0, y1, r, v, ks, g, pw, tm=512):
    T, dr = r.shape
    row = lambda i: (i, 0)
    const = lambda i: (0, 0)
    s1 = pl.BlockSpec((tm, dr), row)
    vec = pl.BlockSpec((1, dr), const)
    return pl.pallas_call(
        functools.partial(_rwkv_post_kernel, hd=pw["hd"]),
        grid=(T // tm,),
        in_specs=[s1, s1, s1, s1, s1, s1, vec, vec, vec, pl.BlockSpec((dr, dr), const)],
        out_specs=s1,
        out_shape=SDS((T, dr), F32),
        compiler_params=_cparams(("parallel",)), name="rwkv_post")(
            y0, y1, r, v, ks, g, pw["r_k"], pw["ln_w"], pw["ln_b"], pw["hsum"])


def _rwkv_mixer(z, B, L, pw):
    r, v, kk, g, ks, lw, kd, bd = _rwkv_pre(z, B, L, pw)
    y0, y1 = _rwkv_chunk(r, v, kk, lw, kd, bd, B, L, pw["heads"], pw["hd"])
    return _rwkv_post(y0, y1, r, v, ks, g, pw)


def _s5_scan_kernel(u_ref, win_ref, are_ref, aim_ref, cout_ref, y_ref, x_s, st_s, *, ns):
    d = pl.program_id(0)
    c = pl.program_id(2)
    lc, bs, cu = u_ref.shape

    @pl.when(c == 0)
    def _():
        st_s[...] = jnp.zeros_like(st_s)

    u = u_ref[...].reshape(lc * bs, cu)
    x_s[...] = _bdot(u, win_ref[...])
    are = jnp.broadcast_to(are_ref[...], (bs, ns))
    aim = jnp.broadcast_to(aim_ref[...], (bs, ns))

    def body(i, carry):
        xr, xi = carry
        t = jnp.where(d == 0, i, lc - 1 - i)
        off = pl.multiple_of(t * bs, bs)
        ir = x_s[pl.ds(off, bs), 0:ns]
        ii = x_s[pl.ds(off, bs), ns:2 * ns]
        nr = are * xr - aim * xi + ir
        ni = are * xi + aim * xr + ii
        x_s[pl.ds(off, bs), 0:ns] = nr
        x_s[pl.ds(off, bs), ns:2 * ns] = ni
        return nr, ni

    xr, xi = lax.fori_loop(0, lc, body, (st_s[0], st_s[1]))
    st_s[0] = xr
    st_s[1] = xi
    y = _bdot(x_s[...], cout_ref[...])
    y_ref[...] = y.reshape(lc, bs, y_ref.shape[-1])


def _s5_scan(u3, sw, lc=64):
    L, B, cu = u3.shape
    bs = V7X_SUBLANES
    ns = sw["ns"]
    nl = L // lc

    def tix(d, c):
        return jnp.where(d == 0, c, nl - 1 - c)

    return pl.pallas_call(
        functools.partial(_s5_scan_kernel, ns=ns),
        grid=(2, B // bs, nl),
        in_specs=[pl.BlockSpec((lc, bs, cu), lambda d, b, c: (tix(d, c), b, 0)),
                  pl.BlockSpec((None, cu, 2 * ns), lambda d, b, c: (d, 0, 0)),
                  pl.BlockSpec((None, 1, ns), lambda d, b, c: (d, 0, 0)),
                  pl.BlockSpec((None, 1, ns), lambda d, b, c: (d, 0, 0)),
                  pl.BlockSpec((2 * ns, cu), lambda d, b, c: (0, 0))],
        out_specs=pl.BlockSpec((None, lc, bs, cu), lambda d, b, c: (d, tix(d, c), b, 0)),
        out_shape=SDS((2, L, B, cu), F32),
        scratch_shapes=[pltpu.VMEM((lc * bs, 2 * ns), F32), pltpu.VMEM((2, bs, ns), F32)],
        compiler_params=_cparams(("parallel", "parallel", "arbitrary")), name="s5_scan")(
            u3, sw["win"], sw["a_re"], sw["a_im"], sw["cout"])


def _gelu_tanh(x):
    return 0.5 * x * (1.0 + jnp.tanh(math.sqrt(2.0 / math.pi) * (x + 0.044715 * (x * x * x))))


def _s5_post_kernel(y_ref, u_ref, d_ref, wg_ref, g_ref, o_ref):
    u = u_ref[...]
    y = y_ref[0] + y_ref[1] + d_ref[...] * u
    h = _gelu_tanh(y)
    out = h * _sigmoid(_bdot(h, wg_ref[...]))
    o_ref[...] = _rms(out, g_ref[...])


def _s5_post(y2, u_tm, B, L, sw, tl=512):
    cu = sw["cu"]
    nl = L // tl
    const = lambda b, l: (0, 0)
    return pl.pallas_call(
        _s5_post_kernel,
        grid=(B, nl),
        in_specs=[pl.BlockSpec((2, tl, cu), lambda b, l: (0, l, b)),
                  pl.BlockSpec((tl, cu), lambda b, l: (l, b)),
                  pl.BlockSpec((1, cu), const), pl.BlockSpec((cu, cu), const), pl.BlockSpec((1, cu), const)],
        out_specs=pl.BlockSpec((tl, cu), lambda b, l: (b * nl + l, 0)),
        out_shape=SDS((B * L, cu), F32),
        compiler_params=_cparams(("parallel", "parallel")), name="s5_post")(
            y2.reshape(2, L, B * cu), u_tm, sw["d"], sw["w_glu"], sw["norm_g"])


def _s5_mixer(u_tm, B, L, sw):
    cu = sw["cu"]
    y2 = _s5_scan(u_tm.reshape(L, B, cu), sw)
    return _s5_post(y2, u_tm, B, L, sw)


def _hy_pre_kernel(z_ref, zp_ref, zn_ref, cw_ref, cb_ref, v_o, x1_o, x2_o, *, dh):
    z = z_ref[...]
    prev_row, next_row = _halo_rows(zp_ref, zn_ref)
    zprev, znext = _shift_rows(z, prev_row, next_row)
    zc = cw_ref[0:1, :] * zprev + cw_ref[1:2, :] * z + cw_ref[2:3, :] * znext + cb_ref[...]
    v_o[...] = zc[:, 0:dh]
    x1_o[...] = zc[:, dh:2 * dh]
    x2_o[...] = zc[:, 2 * dh:3 * dh]


def _hy_pre(z, B, L, hw, tm=512):
    T, ch = z.shape
    dh = hw["dh"]
    nl = L // tm
    row = lambda b, l: (b * nl + l, 0)
    const = lambda b, l: (0, 0)
    prev, nxt = _halo_specs(tm, ch, nl, T)
    o = pl.BlockSpec((tm, dh), row)
    return pl.pallas_call(
        functools.partial(_hy_pre_kernel, dh=dh),
        grid=(B, nl),
        in_specs=[pl.BlockSpec((tm, ch), row), prev, nxt,
                  pl.BlockSpec((3, ch), const), pl.BlockSpec((1, ch), const)],
        out_specs=[o, o, o],
        out_shape=[SDS((T, dh), F32)] * 3,
        compiler_params=_cparams(("parallel", "parallel")), name="hy_pre")(z, z, z, hw["conv_w"], hw["conv_b"])


def _hy_filter_kernel(ft_ref, w1_ref, b1_ref, w2_ref, b2_ref, w3_ref, b3_ref, fr_ref, dl_ref, bm_ref,
                      h_o, s_o, *, L, tl):
    i = pl.program_id(0)
    fr = fr_ref[...]
    h = jnp.sin(fr * (_dot3(ft_ref[...], w1_ref[...]) + b1_ref[...]))
    h = jnp.sin(fr * (_dot3(h, w2_ref[...]) + b2_ref[...]))
    h = _dot3(h, w3_ref[...]) + b3_ref[...]
    pos = (i * tl + lax.broadcasted_iota(I32, (tl, 1), 0)).astype(F32)
    t = pos * (1.0 / (L - 1))
    h = h * jnp.exp(-t * dl_ref[...])
    h_o[...] = h
    keep = jnp.where((pos == 0.0) & (bm_ref[...] > 0.5), 0.0, 1.0)
    part = jnp.sum(jnp.abs(h) * keep, axis=0, keepdims=True)

    @pl.when(i == 0)
    def _():
        s_o[...] = jnp.zeros_like(s_o)

    s_o[...] += part


def _hy_filter(L, hw, tl=256):
    feats = _hyena_features(L)
    ncol = hw["f_w3"].shape[1]
    fw = hw["f_w1"].shape[1]
    const = lambda i: (0, 0)
    full = lambda a: pl.BlockSpec(a.shape, const)
    args = [hw["f_w1"], hw["f_b1"], hw["f_w2"], hw["f_b2"], hw["f_w3"], hw["f_b3"], hw["f_freq"],
            hw["abs_deltas"], hw["bwd_mask"]]
    return pl.pallas_call(
        functools.partial(_hy_filter_kernel, L=L, tl=tl),
        grid=(L // tl,),
        in_specs=[pl.BlockSpec((tl, feats.shape[1]), lambda i: (i, 0))] + [full(a) for a in args],
        out_specs=[pl.BlockSpec((tl, ncol), lambda i: (i, 0)), pl.BlockSpec((1, ncol), const)],
        out_shape=[SDS((L, ncol), F32), SDS((1, ncol), F32)],
        compiler_params=_cparams(("arbitrary",)), name="hy_filter")(feats, *args)


def _hyena_features(L):
    emb_bands = 16
    t = jnp.linspace(0.0, 1.0, L, dtype=F32)[:, None]
    w = (2.0 * math.pi / L) * jnp.arange(L, dtype=F32)[:, None]
    f = jnp.linspace(1e-4, emb_bands - 1, emb_bands, dtype=F32)[None, :]
    feats = jnp.concatenate([t, jnp.cos(f * w), -jnp.sin(f * w)], axis=-1)
    return jnp.pad(feats, ((0, 0), (0, V7X_LANES - feats.shape[1])))


def _dft_tables(N):
    n2 = DFT_INNER
    n1 = N // n2
    two_pi = 2.0 * math.pi

    def cs(num, den):
        ang = (num % den).astype(F32) * (two_pi / den)
        return jnp.cos(ang), jnp.sin(ang)

    f1 = jnp.arange(n1, dtype=I32)
    c1, s1 = cs(f1[:, None] * f1[None, :], n1)
    fwd_full = jnp.concatenate([jnp.concatenate([c1, s1], 1), jnp.concatenate([-s1, c1], 1)], 0)
    h = n1 // 2
    fwd_half = jnp.concatenate([jnp.concatenate([c1[:, :h], s1[:, :h]], 1),
                                jnp.concatenate([-s1[:, :h], c1[:, :h]], 1)], 0)
    ci, si = c1[:h, :], s1[:h, :]
    inv_half = jnp.concatenate([jnp.concatenate([ci, -si], 1), jnp.concatenate([si, ci], 1)], 0) * (1.0 / N)
    f2 = jnp.arange(n2, dtype=I32)
    freq = f1[:, None, None] + n1 * f2[None, :, None]
    c2, s2 = cs(freq * f2[None, None, :], N)
    b_fwd = jnp.concatenate([jnp.concatenate([c2, s2], 2), jnp.concatenate([-s2, c2], 2)], 1)
    b_inv = jnp.swapaxes(b_fwd, 1, 2)
    return dict(n1=n1, n2=n2, fwd_full=fwd_full, fwd_half=fwd_half, inv_half=inv_half, b_fwd=b_fwd, b_inv=b_inv)


def _lead_kernel(*refs, has_inv, has_fwd, has_scale, precise):
    refs = list(refs)
    mm = _dot3 if precise else _bdot
    if has_inv:
        yp_ref, inv_ref, xga_ref, xgb_ref, ua_ref, ub_ref, bias_ref = refs[:7]
        refs = refs[7:]
        yp = yp_ref[...]
        ystk = yp.reshape(yp.shape[0] * yp.shape[1], yp.shape[2])
        y = mm(inv_ref[...], ystk)
        k = y.shape[0] // 2
        bias = bias_ref[...]
        za = xga_ref[...] * (y[:k] + bias * ua_ref[...])
        zb = xgb_ref[...] * (y[k:] + bias * ub_ref[...])
    else:
        xa_ref, xb_ref = refs[:2]
        refs = refs[2:]
        za, zb = xa_ref[...], xb_ref[...]
        if has_scale:
            sc = refs[0][...]
            refs = refs[1:]
            za, zb = za * sc, zb * sc
    if has_fwd:
        fwd_ref = refs[0]
        refs = refs[1:]
    if has_inv:
        refs[0][...] = za
        refs[1][...] = zb
        refs = refs[2:]
    if has_fwd:
        y_o = refs[0]
        yy = mm(fwd_ref[...], jnp.concatenate([za, zb], axis=0))
        y_o[...] = yy.reshape(y_o.shape).astype(y_o.dtype)


def _lead_call(*, pairs, n1, kin, lanes, wl, inv=None, fwd=None, xa=None, xb_off=None, scale=None,
               gate=None, precise=False, name="hy_lead"):
    nj = lanes // wl
    in_specs, args = [], []
    out_specs, out_shape = [], []
    has_inv = gate is not None
    if has_inv:
        kh = gate["xg"].shape[1]
        off = gate["off"]
        sa = pl.BlockSpec((None, kh, wl), lambda p, j: (p, 0, j))
        sb = pl.BlockSpec((None, kh, wl), lambda p, j: (p + off, 0, j))
        in_specs += [pl.BlockSpec((None, 2, n1, wl), lambda p, j: (p, 0, 0, j)),
                     pl.BlockSpec(inv.shape, lambda p, j: (0, 0)), sa, sb, sa, sb,
                     pl.BlockSpec((1, wl), lambda p, j: (0, j))]
        args += [gate["yp"], inv, gate["xg"], gate["xg"], gate["u"], gate["u"], gate["bias"]]
    else:
        in_specs += [pl.BlockSpec((None, kin, wl), lambda p, j: (p, 0, j)),
                     pl.BlockSpec((None, kin, wl), lambda p, j: (p + xb_off, 0, j))]
        args += [xa, xa]
        if scale is not None:
            in_specs.append(pl.BlockSpec((1, wl), lambda p, j: (0, j)))
            args.append(scale)
    if fwd is not None:
        in_specs.append(pl.BlockSpec(fwd.shape, lambda p, j: (0, 0)))
        args.append(fwd)
    if has_inv:
        out_specs += [sa, sa]
        out_shape += [SDS((pairs, kh, lanes), F32)] * 2
    if fwd is not None:
        out_specs.append(pl.BlockSpec((None, 2, n1, wl), lambda p, j: (p, 0, 0, j)))
        out_shape.append(SDS((pairs, 2, n1, lanes), F32 if precise else BF16))
    return pl.pallas_call(
        functools.partial(_lead_kernel, has_inv=has_inv, has_fwd=fwd is not None,
                          has_scale=scale is not None, precise=precise),
        grid=(pairs, nj), in_specs=in_specs, out_specs=out_specs, out_shape=out_shape,
        compiler_params=_cparams(("parallel", "parallel")), name=name)(*args)


def _inner_kernel(y_ref, bf_ref, *rest, f1t, n2, has_inv, precise):
    mm = _dot3 if precise else _bdot
    if has_inv:
        k_ref, bi_ref, o_ref = rest
    else:
        (o_ref,) = rest
    for i in range(f1t):
        ys = jnp.concatenate([y_ref[0, i], y_ref[1, i]], axis=0)
        x = mm(bf_ref[i], ys)
        xr, xi = x[:n2], x[n2:]
        if has_inv:
            kr, ki = k_ref[0, i], k_ref[1, i]
            zr = xr * kr - xi * ki
            zi = xr * ki + xi * kr
            yp = mm(bi_ref[i], jnp.concatenate([zr, zi], axis=0))
            o_ref[0, i] = yp[:n2].astype(o_ref.dtype)
            o_ref[1, i] = yp[n2:].astype(o_ref.dtype)
        else:
            o_ref[0, i] = xr
            o_ref[1, i] = xi


def _inner_call(y5, b_fwd, b_inv=None, kf=None, kblk=0, precise=False, f1t=8, name="hy_inner"):
    P, _, n1, n2, ch = y5.shape
    has_inv = kf is not None
    ys = pl.BlockSpec((None, 2, f1t, n2, ch), lambda j, p: (p, 0, j, 0, 0))
    ts = pl.BlockSpec((f1t, 2 * n2, 2 * n2), lambda j, p: (j, 0, 0))
    in_specs = [ys, ts]
    args = [y5, b_fwd]
    if has_inv:
        in_specs += [pl.BlockSpec((2, f1t, n2, ch), lambda j, p: (0, j, 0, kblk)), ts]
        args += [kf, b_inv]
    return pl.pallas_call(
        functools.partial(_inner_kernel, f1t=f1t, n2=n2, has_inv=has_inv, precise=precise),
        grid=(n1 // f1t, P), in_specs=in_specs, out_specs=ys, out_shape=SDS(y5.shape, y5.dtype),
        compiler_params=_cparams(("parallel", "parallel")), name=name)(*args)


def _hy_spectrum(L, hw, tabs):
    N = 2 * L
    n1, n2 = tabs["n1"], tabs["n2"]
    dh = hw["dh"]
    hwin, asum = _hy_filter(L, hw)
    ks, ss = [], []
    for o in range(2):
        kf = hwin[:, (2 * o) * dh:(2 * o + 1) * dh]
        kb = hwin[:, (2 * o + 1) * dh:(2 * o + 2) * dh]
        ks.append(jnp.concatenate([kf, jnp.zeros((1, dh), F32), kb[:0:-1]], axis=0))
        ss.append(asum[:, (2 * o) * dh:(2 * o + 1) * dh] + asum[:, (2 * o + 1) * dh:(2 * o + 2) * dh])
    k2 = jnp.concatenate(ks, axis=1)
    inv_mass = 1.0 / jnp.concatenate(ss, axis=1)
    lanes = n2 * 2 * dh
    kin = jnp.stack([k2.reshape(n1, lanes), jnp.zeros((n1, lanes), F32)])
    (y,) = _lead_call(pairs=1, n1=n1, kin=n1, lanes=lanes, wl=2048, fwd=tabs["fwd_full"], xa=kin, xb_off=1,
                      scale=jnp.tile(inv_mass, (1, n2)), precise=True, name="hy_filter_lead")
    kf = _inner_call(y.reshape(1, 2, n1, n2, 2 * dh), tabs["b_fwd"], precise=True, name="hy_filter_inner")
    return kf[0]


def _hyena_mixer(z, B, L, hw, tabs, kf):
    dh = hw["dh"]
    n1, n2 = tabs["n1"], tabs["n2"]
    kh = n1 // 2
    lanes = n2 * dh
    P = B // 2
    wl = 2048
    v, x1, x2 = _hy_pre(z, B, L, hw)
    view = lambda a: a.reshape(B, kh, lanes)
    v3, x13, x23 = view(v), view(x1), view(x2)
    bias = [jnp.tile(hw["bias"][o:o + 1], (1, n2)) for o in range(2)]
    (y,) = _lead_call(pairs=P, n1=n1, kin=kh, lanes=lanes, wl=wl, fwd=tabs["fwd_half"], xa=v3, xb_off=P,
                      name="hy_lead0")
    yp = _inner_call(y.reshape(P, 2, n1, n2, dh), tabs["b_fwd"], tabs["b_inv"], kf, kblk=0, name="hy_inner0")
    z1a, z1b, y = _lead_call(pairs=P, n1=n1, kin=kh, lanes=lanes, wl=wl, inv=tabs["inv_half"], fwd=tabs["fwd_half"],
                             gate=dict(yp=yp.reshape(P, 2, n1, lanes), xg=x13, u=v3, bias=bias[0], off=P),
                             name="hy_lead1")
    z1 = jnp.concatenate([z1a, z1b], axis=0)
    yp = _inner_call(y.reshape(P, 2, n1, n2, dh), tabs["b_fwd"], tabs["b_inv"], kf, kblk=1, name="hy_inner1")
    ya, yb = _lead_call(pairs=P, n1=n1, kin=kh, lanes=lanes, wl=wl, inv=tabs["inv_half"],
                        gate=dict(yp=yp.reshape(P, 2, n1, lanes), xg=x23, u=z1, bias=bias[1], off=P),
                        name="hy_lead2")
    return jnp.concatenate([ya, yb], axis=0).reshape(B * L, dh)


def _outproj_kernel(yr_ref, ys_ref, yh_ref, x_ref, wo_ref, gh_ref, gf_ref, wr_ref, xo_ref, xn_ref, aff_ref,
                    *, dr, ds):
    yh = _rms(yh_ref[...], gh_ref[...])
    acc = _bdot(yr_ref[...], wo_ref[0:dr, :])
    acc += _bdot(ys_ref[...], wo_ref[dr:dr + ds, :])
    acc += _bdot(yh, wo_ref[dr + ds:, :])
    xnew = x_ref[...] + acc
    xo_ref[...] = xnew
    xn = _rms(xnew, gf_ref[...])
    xn_ref[...] = xn
    logits = _dot3_t(wr_ref[...], xn)
    m = jnp.max(logits, axis=0, keepdims=True)
    e = jnp.exp(logits - m)
    aff_ref[...] = e / jnp.sum(e, axis=0, keepdims=True)


def _out_proj(yr, ys, yh, x, lw, tm=512):
    T, D = x.shape
    dr, ds, dh = yr.shape[1], ys.shape[1], yh.shape[1]
    E = lw["w_router_t"].shape[0]
    row = lambda i: (i, 0)
    const = lambda i: (0, 0)
    return pl.pallas_call(
        functools.partial(_outproj_kernel, dr=dr, ds=ds),
        grid=(T // tm,),
        in_specs=[pl.BlockSpec((tm, dr), row), pl.BlockSpec((tm, ds), row), pl.BlockSpec((tm, dh), row),
                  pl.BlockSpec((tm, D), row), pl.BlockSpec((D, D), const), pl.BlockSpec((1, dh), const),
                  pl.BlockSpec((1, D), const), pl.BlockSpec((E, D), const)],
        out_specs=[pl.BlockSpec((tm, D), row), pl.BlockSpec((tm, D), row), pl.BlockSpec((E, tm), lambda i: (0, i))],
        out_shape=[SDS((T, D), F32), SDS((T, D), F32), SDS((E, T), F32)],
        compiler_params=_cparams(("parallel",)), name="out_proj")(
            yr, ys, yh, x, lw["w_out"], lw["hy"]["norm_g"], lw["norm_ffn_g"], lw["w_router_t"])


def _select_prefix_kernel(aff_ref, incl_ref, off_ref, *, cap):
    aff = aff_ref[...]
    nt = aff.shape[0]
    bits = pltpu.bitcast(aff, I32)

    def body(i, thr):
        cand = thr | jnp.left_shift(jnp.int32(1), 30 - i)
        cnt = jnp.sum(jnp.where(bits >= cand, 1.0, 0.0))
        return jnp.where(cnt >= cap, cand, thr)

    thr = lax.fori_loop(0, 31, body, jnp.int32(0))
    gt = bits > thr
    eq = bits == thr
    need = cap - jnp.sum(jnp.where(gt, 1.0, 0.0))

    li = lax.broadcasted_iota(I32, (V7X_LANES, V7X_LANES), 0)
    mi = lax.broadcasted_iota(I32, (V7X_LANES, V7X_LANES), 1)
    upper = jnp.where(li <= mi, 1.0, 0.0).astype(BF16)
    ones = jnp.ones((V7X_LANES, V7X_LANES), BF16)
    ri = lax.broadcasted_iota(I32, (nt, nt), 0)
    ci = lax.broadcasted_iota(I32, (nt, nt), 1)
    lower = jnp.where(ci < ri, 1.0, 0.0).astype(BF16)
    dot = functools.partial(jnp.dot, preferred_element_type=F32)

    def prefix(m):
        mb = m.astype(BF16)
        incl = dot(mb, upper)
        tot = dot(mb, ones)
        return incl, dot(lower, tot.astype(BF16))

    eqf = jnp.where(eq, 1.0, 0.0)
    incl_eq, off_eq = prefix(eqf)
    rank_eq = off_eq + incl_eq - eqf
    sel = jnp.where(gt, 1.0, jnp.where(eq & (rank_eq < need), 1.0, 0.0))
    incl, off = prefix(sel)
    incl_ref[...] = incl
    off_ref[...] = off


def _select_index_kernel(incl_ref, off_ref, aff_ref, idx_ref, gate_ref, *, sb):
    j0 = pl.program_id(1) * sb
    incl = incl_ref[...]
    off = off_ref[...]
    nt = incl.shape[0]
    off_col = off[:, 0:1]
    end_col = off_col + incl[:, V7X_LANES - 1:V7X_LANES]
    j = (j0 + lax.broadcasted_iota(I32, (1, sb), 1)).astype(F32)
    tile_id = jnp.sum(jnp.where(end_col <= j, 1.0, 0.0), axis=0, keepdims=True)
    ti = lax.broadcasted_iota(I32, (nt, sb), 0).astype(F32)
    onehot = jnp.where(ti == tile_id, 1.0, 0.0)
    excl = jnp.sum(onehot * off_col, axis=0, keepdims=True)
    rj = j - excl
    ohb = onehot.astype(BF16)
    dot = functools.partial(jnp.dot, preferred_element_type=F32)
    rows = dot(incl.T.astype(BF16), ohb)
    local = jnp.sum(jnp.where(rows <= rj, 1.0, 0.0), axis=0, keepdims=True)
    idx_ref[...] = (tile_id * V7X_LANES + local).astype(I32)
    ah, am, al = _split3(aff_ref[...].T)
    arows = dot(ah, ohb) + dot(am, ohb) + dot(al, ohb)
    lane = lax.broadcasted_iota(I32, (V7X_LANES, sb), 0).astype(F32)
    gate_ref[...] = jnp.sum(jnp.where(lane == local, arows, 0.0), axis=0, keepdims=True)


def _moe_select(aff_t, cap, sb=1024):
    E, T = aff_t.shape
    nt = T // V7X_LANES
    aff3 = aff_t.reshape(E, nt, V7X_LANES)
    s3 = pl.BlockSpec((None, nt, V7X_LANES), lambda e: (e, 0, 0))
    incl, off = pl.pallas_call(
        functools.partial(_select_prefix_kernel, cap=cap),
        grid=(E,), in_specs=[s3], out_specs=[s3, s3],
        out_shape=[SDS((E, nt, V7X_LANES), F32)] * 2,
        compiler_params=_cparams(("parallel",)), name="moe_select_prefix")(aff3)
    s3b = pl.BlockSpec((None, nt, V7X_LANES), lambda e, j: (e, 0, 0))
    so = pl.BlockSpec((None, 1, sb), lambda e, j: (e, 0, j))
    idx, gate = pl.pallas_call(
        functools.partial(_select_index_kernel, sb=sb),
        grid=(E, cap // sb), in_specs=[s3b, s3b, s3b], out_specs=[so, so],
        out_shape=[SDS((E, 1, cap), I32), SDS((E, 1, cap), F32)],
        compiler_params=_cparams(("parallel", "parallel")), name="moe_select_index")(incl, off, aff3)
    return idx, gate


def _moe_ffn_kernel(idx_hbm, gate_ref, xn_hbm, acc_in_hbm, wg_ref, wu_ref, wd_ref, acc_hbm,
                    idx_s, xbuf, abuf, sem_i, sem_x, sem_a, sem_s, *, m, nt):
    del acc_in_hbm
    e = pl.program_id(0)
    i = pl.program_id(1)
    cp = pltpu.make_async_copy(idx_hbm.at[e * nt + i], idx_s, sem_i)
    cp.start()
    cp.wait()

    def issue(r, carry):
        t = idx_s[r]
        pltpu.make_async_copy(xn_hbm.at[t], xbuf.at[r], sem_x).start()
        pltpu.make_async_copy(acc_hbm.at[t], abuf.at[r], sem_a).start()
        return carry

    lax.fori_loop(0, m, issue, 0, unroll=8)
    pltpu.make_async_copy(xn_hbm.at[pl.ds(0, m)], xbuf, sem_x).wait()
    pltpu.make_async_copy(acc_hbm.at[pl.ds(0, m)], abuf, sem_a).wait()

    x = xbuf[...].astype(BF16)
    hg = jnp.dot(x, wg_ref[...], preferred_element_type=F32)
    hu = jnp.dot(x, wu_ref[...], preferred_element_type=F32)
    h = hg * _sigmoid(hg) * hu
    y = jnp.dot(h.astype(BF16), wd_ref[...], preferred_element_type=F32)
    g_col = jnp.broadcast_to(gate_ref[...], (V7X_LANES, m)).T[:, 0:1]
    abuf[...] = abuf[...] + y * g_col

    def scatter(r, carry):
        t = idx_s[r]
        pltpu.make_async_copy(abuf.at[r], acc_hbm.at[t], sem_s).start()
        return carry

    lax.fori_loop(0, m, scatter, 0, unroll=8)
    pltpu.make_async_copy(abuf, acc_hbm.at[pl.ds(0, m)], sem_s).wait()


def _moe_ffn(idx, gate, xn, acc, mw, m=512):
    T, D = xn.shape
    E, _, cap = idx.shape
    nt = cap // m
    F = mw["w_gate"].shape[2]
    idx2 = idx.reshape(E * nt, m)
    gate3 = gate.reshape(E * nt, 1, m)
    wspec = lambda a, b: pl.BlockSpec((None, a, b), lambda e, i: (e, 0, 0))
    any_spec = pl.BlockSpec(memory_space=pl.ANY)
    return pl.pallas_call(
        functools.partial(_moe_ffn_kernel, m=m, nt=nt),
        grid=(E, nt),
        in_specs=[any_spec, pl.BlockSpec((None, 1, m), lambda e, i: (e * nt + i, 0, 0)), any_spec, any_spec,
                  wspec(D, F), wspec(D, F), wspec(F, D)],
        out_specs=any_spec,
        out_shape=SDS((T, D), F32),
        input_output_aliases={3: 0},
        scratch_shapes=[pltpu.SMEM((m,), I32), pltpu.VMEM((m, D), F32), pltpu.VMEM((m, D), F32),
                        pltpu.SemaphoreType.DMA, pltpu.SemaphoreType.DMA, pltpu.SemaphoreType.DMA,
                        pltpu.SemaphoreType.DMA],
        compiler_params=_cparams(("arbitrary", "arbitrary")), name="moe_ffn")(
            idx2, gate3, xn, acc, mw["w_gate"], mw["w_up"], mw["w_down"])


def _moe(xnew, xn, aff_t, mw):
    T = xn.shape[0]
    E = aff_t.shape[0]
    cap = EC_CAPACITY_FACTOR * T // E
    idx, gate = _moe_select(aff_t, cap)
    return _moe_ffn(idx, gate, xn, xnew, mw)


def _norm_kernel(x_ref, g_ref, o_ref):
    o_ref[...] = _rms(x_ref[...], g_ref[...])


def _final_norm(x, g, tm=512):
    T, D = x.shape
    return pl.pallas_call(
        _norm_kernel, grid=(T // tm,),
        in_specs=[pl.BlockSpec((tm, D), lambda i: (i, 0)), pl.BlockSpec((1, D), lambda i: (0, 0))],
        out_specs=pl.BlockSpec((tm, D), lambda i: (i, 0)), out_shape=SDS((T, D), F32),
        compiler_params=_cparams(("parallel",)), name="final_norm")(x, g)


def _block_diag_rows(blocks):
    n, r, c = blocks.shape
    eye = jnp.eye(n, dtype=blocks.dtype)
    return jnp.einsum("nrc,nm->nrmc", blocks, eye).reshape(n * r, n * c)


def _rwkv_weights(i, p):
    heads, hd = p["rwkv_r_k"].shape[1:]
    dr = heads * hd
    row = lambda a: a.reshape(1, -1)
    head_id = jnp.arange(dr) // hd
    return dict(
        dr=dr, heads=heads, hd=hd,
        mu_prev=row(p["rwkv_mu_prev"][i]), mu_next=row(p["rwkv_mu_next"][i]),
        w0=row(p["rwkv_w0"][i]), wup=_block_diag_rows(p["rwkv_w_up"][i]).astype(BF16),
        a0=row(p["rwkv_a0"][i]), aup=_block_diag_rows(p["rwkv_a_up"][i]).astype(BF16),
        gup=p["rwkv_g_up"][i].astype(BF16),
        k_k=row(p["rwkv_k_k"][i]), k_a=row(p["rwkv_k_a"][i]), r_k=row(p["rwkv_r_k"][i]),
        ln_w=row(p["rwkv_ln_w"][i]), ln_b=row(p["rwkv_ln_b"][i]),
        hsum=(head_id[:, None] == head_id[None, :]).astype(BF16))


def _s5_weights(i, p):
    lam_re, lam_im, log_dt = p["s5_lam_re"][i], p["s5_lam_im"][i], p["s5_log_dt"][i]
    b_re, b_im, c_re, c_im = p["s5_b_re"][i], p["s5_b_im"][i], p["s5_c_re"][i], p["s5_c_im"][i]
    G, P, H = b_re.shape
    l_re = jnp.minimum(lam_re, -1e-4)
    dt = jnp.exp(log_dt)[..., None]
    mag = jnp.exp(l_re * dt)
    ab_re = mag * jnp.cos(lam_im * dt)
    ab_im = mag * jnp.sin(lam_im * dt)
    den = l_re * l_re + lam_im * lam_im
    n_re = ab_re - 1.0
    f_re = (n_re * l_re + ab_im * lam_im) / den
    f_im = (ab_im * l_re - n_re * lam_im) / den
    w_re = jnp.swapaxes(f_re[..., None] * b_re - f_im[..., None] * b_im, 2, 3)
    w_im = jnp.swapaxes(f_re[..., None] * b_im + f_im[..., None] * b_re, 2, 3)
    win = jnp.stack([jnp.concatenate([_block_diag_rows(w_re[d]), _block_diag_rows(w_im[d])], axis=1)
                     for d in range(2)]).astype(BF16)
    cout = jnp.concatenate([_block_diag_rows(jnp.swapaxes(c_re, 1, 2)),
                            _block_diag_rows(jnp.swapaxes(-c_im, 1, 2))], axis=0).astype(BF16)
    ns = G * P
    return dict(ns=ns, cu=G * H, win=win, cout=cout,
                a_re=ab_re.reshape(2, 1, ns), a_im=ab_im.reshape(2, 1, ns),
                d=p["s5_d"][i].reshape(1, -1), w_glu=p["s5_w_glu"][i].astype(BF16),
                norm_g=p["s5_norm_g"][i].reshape(1, -1))


def _hyena_weights(i, p):
    dh = p["hy_norm_g"].shape[1]
    emb = p["hy_f_w1"].shape[1]
    w1 = jnp.pad(p["hy_f_w1"][i], ((0, V7X_LANES - emb), (0, 0)))
    deltas = jnp.linspace(math.log(HYENA_TARGET) / HYENA_FAST_DECAY, math.log(HYENA_TARGET) / HYENA_SLOW_DECAY,
                          dh, dtype=F32)
    ncol = p["hy_f_w3"].shape[2]
    col_dir = (jnp.arange(ncol) // dh) % 2
    row = lambda a: a.reshape(1, -1)
    return dict(dh=dh, conv_w=p["hy_conv_w"][i], conv_b=row(p["hy_conv_b"][i]),
                f_w1=w1, f_b1=row(p["hy_f_b1"][i]), f_w2=p["hy_f_w2"][i], f_b2=row(p["hy_f_b2"][i]),
                f_w3=p["hy_f_w3"][i], f_b3=row(p["hy_f_b3"][i]), f_freq=row(p["hy_f_freq"][i]),
                abs_deltas=jnp.tile(jnp.abs(deltas), ncol // dh).reshape(1, -1),
                bwd_mask=col_dir.astype(F32).reshape(1, -1),
                bias=p["hy_bias"][i], norm_g=row(p["hy_norm_g"][i]))


def _layer_weights(i, p):
    return dict(
        norm_mix_g=p["norm_mix_g"][i].reshape(1, -1), w_in=p["w_in"][i].astype(BF16),
        rwkv=_rwkv_weights(i, p), s5=_s5_weights(i, p), hy=_hyena_weights(i, p),
        w_out=p["w_out"][i].astype(BF16), norm_ffn_g=p["norm_ffn_g"][i].reshape(1, -1),
        w_router_t=p["moe_w_router"][i].T,
        moe=dict(w_gate=p["moe_w_gate"][i].astype(BF16), w_up=p["moe_w_up"][i].astype(BF16),
                 w_down=p["moe_w_down"][i].astype(BF16)))


def _layer(x, B, L, lw, tabs, kf):
    c_r = lw["rwkv"]["mu_prev"].shape[1]
    c_s = lw["s5"]["cu"]
    c_h = 3 * lw["hy"]["dh"]
    z_r, z_s, z_h = _in_proj(x, B, L, lw["norm_mix_g"], lw["w_in"], c_r, c_s, c_h)
    y_r = _rwkv_mixer(z_r, B, L, lw["rwkv"])
    y_s = _s5_mixer(z_s, B, L, lw["s5"])
    y_h = _hyena_mixer(z_h, B, L, lw["hy"], tabs, kf)
    xnew, xn, aff_t = _out_proj(y_r, y_s, y_h, x, lw)
    return _moe(xnew, xn, aff_t, lw["moe"])


def _trunk(x, layers, final_g):
    B, L, D = x.shape
    tabs = _dft_tables(2 * L)
    h = x.reshape(B * L, D)
    for lw in layers:
        kf = _hy_spectrum(L, lw["hy"], tabs)
        h = _layer(h, B, L, lw, tabs, kf)
    return _final_norm(h, final_g.reshape(1, -1)).reshape(B, L, D)


def kernel(x_prompt, x_sample, norm_mix_g, w_in, rwkv_mu_prev, rwkv_mu_next, rwkv_w0, rwkv_w_up, rwkv_a0, rwkv_a_up, rwkv_g_up, rwkv_k_k, rwkv_k_a, rwkv_r_k, rwkv_ln_w, rwkv_ln_b, s5_lam_re, s5_lam_im, s5_log_dt, s5_b_re, s5_b_im, s5_c_re, s5_c_im, s5_d, s5_w_glu, s5_norm_g, hy_conv_w, hy_conv_b, hy_f_w1, hy_f_b1, hy_f_w2, hy_f_b2, hy_f_w3, hy_f_b3, hy_f_freq, hy_bias, hy_norm_g, w_out, norm_ffn_g, moe_w_router, moe_w_gate, moe_w_up, moe_w_down, final_norm_g):
    p = dict(norm_mix_g=norm_mix_g, w_in=w_in, rwkv_mu_prev=rwkv_mu_prev, rwkv_mu_next=rwkv_mu_next,
             rwkv_w0=rwkv_w0, rwkv_w_up=rwkv_w_up, rwkv_a0=rwkv_a0, rwkv_a_up=rwkv_a_up,
             rwkv_g_up=rwkv_g_up, rwkv_k_k=rwkv_k_k, rwkv_k_a=rwkv_k_a, rwkv_r_k=rwkv_r_k,
             rwkv_ln_w=rwkv_ln_w, rwkv_ln_b=rwkv_ln_b, s5_lam_re=s5_lam_re, s5_lam_im=s5_lam_im,
             s5_log_dt=s5_log_dt, s5_b_re=s5_b_re, s5_b_im=s5_b_im, s5_c_re=s5_c_re, s5_c_im=s5_c_im,
             s5_d=s5_d, s5_w_glu=s5_w_glu, s5_norm_g=s5_norm_g, hy_conv_w=hy_conv_w, hy_conv_b=hy_conv_b,
             hy_f_w1=hy_f_w1, hy_f_b1=hy_f_b1, hy_f_w2=hy_f_w2, hy_f_b2=hy_f_b2, hy_f_w3=hy_f_w3,
             hy_f_b3=hy_f_b3, hy_f_freq=hy_f_freq, hy_bias=hy_bias, hy_norm_g=hy_norm_g, w_out=w_out,
             norm_ffn_g=norm_ffn_g, moe_w_router=moe_w_router, moe_w_gate=moe_w_gate,
             moe_w_up=moe_w_up, moe_w_down=moe_w_down)
    layers = [_layer_weights(i, p) for i in range(w_in.shape[0])]
    return (_trunk(x_prompt, layers, final_norm_g), _trunk(x_sample, layers, final_norm_g))
```

```python
import functools
import math

import jax
import jax.numpy as jnp
from jax import lax
from jax.experimental import pallas as pl
from jax.experimental.pallas import tpu as pltpu

F32 = jnp.float32
BF16 = jnp.bfloat16
I32 = jnp.int32
SDS = jax.ShapeDtypeStruct

EPS_NORM = 1e-6
EPS_GN = 64e-5
HYENA_FAST_DECAY = 0.3
HYENA_SLOW_DECAY = 1.5
HYENA_TARGET = 1e-2
EC_CAPACITY_FACTOR = 2

V7X_LANES = 128
V7X_SUBLANES = 8
V7X_VMEM_LIMIT_BYTES = 56 * 1024 * 1024

RWKV_CHUNK = 64
DFT_INNER = 64


def _cparams(sem, vmem=V7X_VMEM_LIMIT_BYTES):
    return pltpu.CompilerParams(dimension_semantics=sem, vmem_limit_bytes=vmem)


def _bdot(a, b):
    return jnp.dot(a.astype(BF16), b.astype(BF16), preferred_element_type=F32)


def _bdot_t(a, b):
    return lax.dot_general(a.astype(BF16), b.astype(BF16), (((1,), (1,)), ((), ())),
                           preferred_element_type=F32)


def _split2(x):
    hi = x.astype(BF16)
    lo = (x - hi.astype(F32)).astype(BF16)
    return hi, lo


def _split3(x):
    hi = x.astype(BF16)
    r1 = x - hi.astype(F32)
    mid = r1.astype(BF16)
    lo = (r1 - mid.astype(F32)).astype(BF16)
    return hi, mid, lo


def _dot3(a, b):
    ah, al = _split2(a)
    bh, bl = _split2(b)
    d = functools.partial(jnp.dot, preferred_element_type=F32)
    return d(ah, bh) + d(ah, bl) + d(al, bh)


def _dot3_t(a, b):
    ah, al = _split2(a)
    bh, bl = _split2(b)
    d = lambda x, y: lax.dot_general(x, y, (((1,), (1,)), ((), ())), preferred_element_type=F32)
    return d(ah, bh) + d(ah, bl) + d(al, bh)


def _dot_exact_rhs(m01, x):
    xh, xl = _split2(x)
    d = functools.partial(jnp.dot, preferred_element_type=F32)
    return d(m01, xh) + d(m01, xl)


def _dot_exact_lhs(x, m01):
    xh, xl = _split2(x)
    d = functools.partial(jnp.dot, preferred_element_type=F32)
    return d(xh, m01) + d(xl, m01)


def _rms(x, g):
    return x * lax.rsqrt(jnp.mean(x * x, axis=-1, keepdims=True) + EPS_NORM) * g


def _sigmoid(x):
    return 1.0 / (1.0 + jnp.exp(-x))


def _softplus(x):
    return jnp.maximum(x, 0.0) + jnp.log(1.0 + jnp.exp(-jnp.abs(x)))


def _shift_rows(z, prev_row, next_row):
    n = z.shape[0]
    rows = lax.broadcasted_iota(I32, z.shape, 0)
    zprev = jnp.where(rows == 0, prev_row, pltpu.roll(z, 1, 0))
    znext = jnp.where(rows == n - 1, next_row, pltpu.roll(z, n - 1, 0))
    return zprev, znext


def _halo_specs(tm, cols, nl, total_rows):
    r8 = tm // V7X_SUBLANES
    last = total_rows // V7X_SUBLANES - 1
    prev = pl.BlockSpec((V7X_SUBLANES, cols), lambda b, l: (jnp.maximum((b * nl + l) * r8 - 1, 0), 0))
    nxt = pl.BlockSpec((V7X_SUBLANES, cols), lambda b, l: (jnp.minimum((b * nl + l + 1) * r8, last), 0))
    return prev, nxt


def _halo_rows(zp_ref, zn_ref):
    l = pl.program_id(1)
    nl = pl.num_programs(1)
    prev_row = jnp.where(l == 0, 0.0, zp_ref[V7X_SUBLANES - 1:V7X_SUBLANES, :])
    next_row = jnp.where(l == nl - 1, 0.0, zn_ref[0:1, :])
    return prev_row, next_row


def _inproj_kernel(x_ref, g_ref, w_ref, zr_ref, zs_ref, zh_ref, *, c_r, c_s):
    h = _rms(x_ref[...], g_ref[...])
    z = _bdot(h, w_ref[...])
    zr_ref[...] = z[:, :c_r]
    zs_ref[...] = z[:, c_r:c_r + c_s]
    zh_ref[...] = z[:, c_r + c_s:]


def _in_proj(x, B, L, g, w, c_r, c_s, c_h, tm=512):
    T, D = x.shape
    nl = L // tm
    return pl.pallas_call(
        functools.partial(_inproj_kernel, c_r=c_r, c_s=c_s),
        grid=(B, nl),
        in_specs=[pl.BlockSpec((tm, D), lambda b, l: (b * nl + l, 0)),
                  pl.BlockSpec((1, D), lambda b, l: (0, 0)),
                  pl.BlockSpec((D, c_r + c_s + c_h), lambda b, l: (0, 0))],
        out_specs=[pl.BlockSpec((tm, c_r), lambda b, l: (b * nl + l, 0)),
                   pl.BlockSpec((tm, c_s), lambda b, l: (l, b)),
                   pl.BlockSpec((tm, c_h), lambda b, l: (b * nl + l, 0))],
        out_shape=[SDS((T, c_r), F32), SDS((L, B * c_s), F32), SDS((T, c_h), F32)],
        compiler_params=_cparams(("parallel", "parallel")), name="in_proj")(x, g, w)


def _rwkv_pre_kernel(z_ref, zp_ref, zn_ref, mup_ref, mun_ref, w0_ref, wup_ref, a0_ref, aup_ref, gup_ref,
                     kk_ref, ka_ref, hs_ref,
                     r_o, v_o, kkn_o, g_o, ks_o, lw_o, kd_o, bd_o, *, dr):
    z = z_ref[...]
    prev_row, next_row = _halo_rows(zp_ref, zn_ref)
    zprev, znext = _shift_rows(z, prev_row, next_row)
    zs = z + mup_ref[...] * (zprev - z) + mun_ref[...] * (znext - z)
    r = zs[:, 0:dr]
    k = zs[:, dr:2 * dr]
    v = zs[:, 2 * dr:3 * dr]
    c3 = 3 * dr
    wd = zs[:, c3:c3 + 128]
    ad = zs[:, c3 + 128:c3 + 256]
    gd = zs[:, c3 + 256:c3 + 384]
    w_raw = w0_ref[...] + _bdot(jnp.tanh(wd), wup_ref[...])
    lw = -jnp.exp(-_softplus(-w_raw) - 0.5)
    a = _sigmoid(a0_ref[...] + _bdot(ad, aup_ref[...]))
    g = _bdot(_sigmoid(gd), gup_ref[...])
    kk = k * kk_ref[...]
    ss = _dot_exact_lhs(kk * kk, hs_ref[...])
    kk = kk / jnp.maximum(jnp.sqrt(ss), 1e-12)
    ka = ka_ref[...]
    r_o[...] = r
    v_o[...] = v
    kkn_o[...] = kk
    g_o[...] = g
    ksum = None
    for d in range(2):
        a_d = a[:, d * dr:(d + 1) * dr]
        k_d = k * (1.0 + (a_d - 1.0) * ka)
        lw_o[d] = lw[:, d * dr:(d + 1) * dr]
        kd_o[d] = k_d
        bd_o[d] = kk * a_d
        ksum = k_d if ksum is None else ksum + k_d
    ks_o[...] = ksum


def _rwkv_pre(z, B, L, pw, tm=256):
    T, cr = z.shape
    dr = pw["dr"]
    nl = L // tm
    row = lambda b, l: (b * nl + l, 0)
    const = lambda b, l: (0, 0)
    prev, nxt = _halo_specs(tm, cr, nl, T)
    full = lambda a: pl.BlockSpec(a.shape, const)
    params = [pw["mu_prev"], pw["mu_next"], pw["w0"], pw["wup"], pw["a0"], pw["aup"], pw["gup"],
              pw["k_k"], pw["k_a"], pw["hsum"]]
    o1 = pl.BlockSpec((tm, dr), row)
    o2 = pl.BlockSpec((2, tm, dr), lambda b, l: (0, b * nl + l, 0))
    return pl.pallas_call(
        functools.partial(_rwkv_pre_kernel, dr=dr),
        grid=(B, nl),
        in_specs=[pl.BlockSpec((tm, cr), row), prev, nxt] + [full(a) for a in params],
        out_specs=[o1] * 5 + [o2] * 3,
        out_shape=[SDS((T, dr), F32)] * 5 + [SDS((2, T, dr), F32)] * 3,
        compiler_params=_cparams(("parallel", "parallel")), name="rwkv_pre")(z, z, z, *params)


def _bmm(a, b):
    return lax.dot_general(a.astype(BF16), b.astype(BF16), (((2,), (1,)), ((0,), (0,))),
                           preferred_element_type=F32)


def _bmm_t(a, b):
    return lax.dot_general(a.astype(BF16), b.astype(BF16), (((2,), (2,)), ((0,), (0,))),
                           preferred_element_type=F32)


def _rwkv_chunk_operands(d, r_ref, v_ref, kk_ref, lw_ref, kd_ref, bd_ref, *, heads, hd):
    C = r_ref.shape[0]
    ti = lax.broadcasted_iota(I32, (C, C), 0)
    si = lax.broadcasted_iota(I32, (C, C), 1)
    dd = (si - ti) if d == 0 else (ti - si)
    lw = lw_ref[...]
    cum = _dot_exact_rhs(jnp.where(dd <= 0, 1.0, 0.0).astype(BF16), lw)
    tot = jnp.sum(lw, axis=0, keepdims=True)
    kd = kd_ref[...]
    bd = bd_ref[...]
    p_inv = jnp.exp(-cum)
    p_end = jnp.exp(tot - cum)

    def split(x):
        return jnp.stack([x[:, h * hd:(h + 1) * hd] for h in range(heads)])

    return dict(
        incl=(dd <= 0)[None], strict=(dd < 0)[None], eye=jnp.where(dd == 0, 1.0, 0.0)[None],
        xq=split(jnp.concatenate([kk_ref[...] * jnp.exp(cum - lw), r_ref[...] * jnp.exp(cum)], axis=0).astype(BF16)),
        kt=split((kd * p_inv).astype(BF16)), bt=split((bd * p_inv).astype(BF16)),
        kb=split(jnp.concatenate([kd * p_end, bd * p_end], axis=0).astype(BF16)),
        v=split(v_ref[...].astype(BF16)), p_tot=split(jnp.exp(tot)))


def _rwkv_chunk_kernel(r0, v0, kk0, lw0, kd0, bd0, r1, v1, kk1, lw1, kd1, bd1, y0, y1, s_ref, *, heads, hd):
    @pl.when(pl.program_id(1) == 0)
    def _():
        s_ref[...] = jnp.zeros_like(s_ref)

    C = r0.shape[0]
    ops = [_rwkv_chunk_operands(0, r0, v0, kk0, lw0, kd0, bd0, heads=heads, hd=hd),
           _rwkv_chunk_operands(1, r1, v1, kk1, lw1, kd1, bd1, heads=heads, hd=hd)]
    dirs = (0, 1)
    a_k = [_bmm_t(ops[d]["xq"], ops[d]["kt"]) for d in dirs]
    a_b = [_bmm_t(ops[d]["xq"], ops[d]["bt"]) for d in dirs]
    a_kq = [jnp.concatenate([jnp.where(ops[d]["strict"], a_k[d][:, :C], 0.0),
                             jnp.where(ops[d]["incl"], a_k[d][:, C:], 0.0)], axis=1) for d in dirs]
    a_rb = [jnp.where(ops[d]["incl"], a_b[d][:, C:], 0.0) for d in dirs]
    npow = [jnp.where(ops[d]["strict"], a_b[d][:, :C], 0.0) for d in dirs]
    tinv = [ops[d]["eye"] - npow[d] for d in dirs]
    for _ in range(int(math.log2(C)) - 1):
        npow = [_bmm(npow[d], npow[d]) for d in dirs]
        tinv = [tinv[d] + _bmm(tinv[d], npow[d]) for d in dirs]
    s_old = [s_ref[d] for d in dirs]
    xs = [_bmm_t(ops[d]["xq"], s_old[d]) for d in dirs]
    av = [_bmm(a_kq[d], ops[d]["v"]) for d in dirs]
    u = [_bmm(tinv[d], -(xs[d][:, :C] + av[d][:, :C])) for d in dirs]
    o = [xs[d][:, C:] + av[d][:, C:] + _bmm(a_rb[d], u[d]) for d in dirs]
    for d, y_ref in zip(dirs, (y0, y1)):
        vu = jnp.concatenate([ops[d]["v"].astype(F32), u[d]], axis=1)
        upd = lax.dot_general(vu.astype(BF16), ops[d]["kb"], (((1,), (1,)), ((0,), (0,))),
                              preferred_element_type=F32)
        s_ref[d] = s_old[d] * ops[d]["p_tot"] + upd
        for h in range(heads):
            y_ref[:, h * hd:(h + 1) * hd] = o[d][h]


def _rwkv_chunk(r, v, kk, lw, kd, bd, B, L, heads, hd):
    T, dr = r.shape
    C = RWKV_CHUNK
    nc = L // C
    f1 = pl.BlockSpec((C, dr), lambda b, c: (b * nc + c, 0))
    b1 = pl.BlockSpec((C, dr), lambda b, c: (b * nc + nc - 1 - c, 0))
    f2 = pl.BlockSpec((None, C, dr), lambda b, c: (0, b * nc + c, 0))
    b2 = pl.BlockSpec((None, C, dr), lambda b, c: (1, b * nc + nc - 1 - c, 0))
    return pl.pallas_call(
        functools.partial(_rwkv_chunk_kernel, heads=heads, hd=hd),
        grid=(B, nc),
        in_specs=[f1, f1, f1, f2, f2, f2, b1, b1, b1, b2, b2, b2],
        out_specs=[f1, b1],
        out_shape=[SDS((T, dr), F32)] * 2,
        scratch_shapes=[pltpu.VMEM((2, heads, hd, hd), F32)],
        compiler_params=_cparams(("parallel", "arbitrary")), name="rwkv_chunk")(
            r, v, kk, lw, kd, bd, r, v, kk, lw, kd, bd)


def _rwkv_post_kernel(y0_ref, y1_ref, r_ref, v_ref, ks_ref, g_ref, rk_ref, lnw_ref, lnb_ref, hs_ref, o_ref, *, hd):
    y = y0_ref[...] + y1_ref[...]
    hs = hs_ref[...]
    mean = _dot_exact_lhs(y, hs) * (1.0 / hd)
    yc = y - mean
    var = _dot_exact_lhs(yc * yc, hs) * (1.0 / hd)
    yn = yc * lax.rsqrt(var + EPS_GN) * lnw_ref[...] + lnb_ref[...]
    bonus = _dot_exact_lhs(r_ref[...] * ks_ref[...] * rk_ref[...], hs) * v_ref[...]
    o_ref[...] = (yn + bonus) * g_ref[...]


def _rwkv_post(y0, y1, r, v, ks, g, pw, tm=512):
    T, dr = r.shape
    row = lambda i: (i, 0)
    const = lambda i: (0, 0)
    s1 = pl.BlockSpec((tm, dr), row)
    vec = pl.BlockSpec((1, dr), const)
    return pl.pallas_call(
        functools.partial(_rwkv_post_kernel, hd=pw["hd"]),
        grid=(T // tm,),
        in_specs=[s1, s1, s1, s1, s1, s1, vec, vec, vec, pl.BlockSpec((dr, dr), const)],
        out_specs=s1,
        out_shape=SDS((T, dr), F32),
        compiler_params=_cparams(("parallel",)), name="rwkv_post")(
            y0, y1, r, v, ks, g, pw["r_k"], pw["ln_w"], pw["ln_b"], pw["hsum"])


def _rwkv_mixer(z, B, L, pw):
    r, v, kk, g, ks, lw, kd, bd = _rwkv_pre(z, B, L, pw)
    y0, y1 = _rwkv_chunk(r, v, kk, lw, kd, bd, B, L, pw["heads"], pw["hd"])
    return _rwkv_post(y0, y1, r, v, ks, g, pw)


def _s5_scan_kernel(u_ref, win_ref, are_ref, aim_ref, cout_ref, y_ref, x_s, st_s, *, ns):
    d = pl.program_id(0)
    c = pl.program_id(2)
    lc, bs, cu = u_ref.shape

    @pl.when(c == 0)
    def _():
        st_s[...] = jnp.zeros_like(st_s)

    u = u_ref[...].reshape(lc * bs, cu)
    x_s[...] = _bdot(u, win_ref[...])
    are = jnp.broadcast_to(are_ref[...], (bs, ns))
    aim = jnp.broadcast_to(aim_ref[...], (bs, ns))

    def body(i, carry):
        xr, xi = carry
        t = jnp.where(d == 0, i, lc - 1 - i)
        off = pl.multiple_of(t * bs, bs)
        ir = x_s[pl.ds(off, bs), 0:ns]
        ii = x_s[pl.ds(off, bs), ns:2 * ns]
        nr = are * xr - aim * xi + ir
        ni = are * xi + aim * xr + ii
        x_s[pl.ds(off, bs), 0:ns] = nr
        x_s[pl.ds(off, bs), ns:2 * ns] = ni
        return nr, ni

    xr, xi = lax.fori_loop(0, lc, body, (st_s[0], st_s[1]))
    st_s[0] = xr
    st_s[1] = xi
    y = _bdot(x_s[...], cout_ref[...])
    y_ref[...] = y.reshape(lc, bs, y_ref.shape[-1])


def _s5_scan(u3, sw, lc=64):
    L, B, cu = u3.shape
    bs = V7X_SUBLANES
    ns = sw["ns"]
    nl = L // lc

    def tix(d, c):
        return jnp.where(d == 0, c, nl - 1 - c)

    return pl.pallas_call(
        functools.partial(_s5_scan_kernel, ns=ns),
        grid=(2, B // bs, nl),
        in_specs=[pl.BlockSpec((lc, bs, cu), lambda d, b, c: (tix(d, c), b, 0)),
                  pl.BlockSpec((None, cu, 2 * ns), lambda d, b, c: (d, 0, 0)),
                  pl.BlockSpec((None, 1, ns), lambda d, b, c: (d, 0, 0)),
                  pl.BlockSpec((None, 1, ns), lambda d, b, c: (d, 0, 0)),
                  pl.BlockSpec((2 * ns, cu), lambda d, b, c: (0, 0))],
        out_specs=pl.BlockSpec((None, lc, bs, cu), lambda d, b, c: (d, tix(d, c), b, 0)),
        out_shape=SDS((2, L, B, cu), F32),
        scratch_shapes=[pltpu.VMEM((lc * bs, 2 * ns), F32), pltpu.VMEM((2, bs, ns), F32)],
        compiler_params=_cparams(("parallel", "parallel", "arbitrary")), name="s5_scan")(
            u3, sw["win"], sw["a_re"], sw["a_im"], sw["cout"])


def _gelu_tanh(x):
    return 0.5 * x * (1.0 + jnp.tanh(math.sqrt(2.0 / math.pi) * (x + 0.044715 * (x * x * x))))


def _s5_post_kernel(y_ref, u_ref, d_ref, wg_ref, g_ref, o_ref):
    u = u_ref[...]
    y = y_ref[0] + y_ref[1] + d_ref[...] * u
    h = _gelu_tanh(y)
    out = h * _sigmoid(_bdot(h, wg_ref[...]))
    o_ref[...] = _rms(out, g_ref[...])


def _s5_post(y2, u_tm, B, L, sw, tl=512):
    cu = sw["cu"]
    nl = L // tl
    const = lambda b, l: (0, 0)
    return pl.pallas_call(
        _s5_post_kernel,
        grid=(B, nl),
        in_specs=[pl.BlockSpec((2, tl, cu), lambda b, l: (0, l, b)),
                  pl.BlockSpec((tl, cu), lambda b, l: (l, b)),
                  pl.BlockSpec((1, cu), const), pl.BlockSpec((cu, cu), const), pl.BlockSpec((1, cu), const)],
        out_specs=pl.BlockSpec((tl, cu), lambda b, l: (b * nl + l, 0)),
        out_shape=SDS((B * L, cu), F32),
        compiler_params=_cparams(("parallel", "parallel")), name="s5_post")(
            y2.reshape(2, L, B * cu), u_tm, sw["d"], sw["w_glu"], sw["norm_g"])


def _s5_mixer(u_tm, B, L, sw):
    cu = sw["cu"]
    y2 = _s5_scan(u_tm.reshape(L, B, cu), sw)
    return _s5_post(y2, u_tm, B, L, sw)


def _hy_pre_kernel(z_ref, zp_ref, zn_ref, cw_ref, cb_ref, v_o, x1_o, x2_o, *, dh):
    z = z_ref[...]
    prev_row, next_row = _halo_rows(zp_ref, zn_ref)
    zprev, znext = _shift_rows(z, prev_row, next_row)
    zc = cw_ref[0:1, :] * zprev + cw_ref[1:2, :] * z + cw_ref[2:3, :] * znext + cb_ref[...]
    v_o[...] = zc[:, 0:dh]
    x1_o[...] = zc[:, dh:2 * dh]
    x2_o[...] = zc[:, 2 * dh:3 * dh]


def _hy_pre(z, B, L, hw, tm=512):
    T, ch = z.shape
    dh = hw["dh"]
    nl = L // tm
    row = lambda b, l: (b * nl + l, 0)
    const = lambda b, l: (0, 0)
    prev, nxt = _halo_specs(tm, ch, nl, T)
    o = pl.BlockSpec((tm, dh), row)
    return pl.pallas_call(
        functools.partial(_hy_pre_kernel, dh=dh),
        grid=(B, nl),
        in_specs=[pl.BlockSpec((tm, ch), row), prev, nxt,
                  pl.BlockSpec((3, ch), const), pl.BlockSpec((1, ch), const)],
        out_specs=[o, o, o],
        out_shape=[SDS((T, dh), F32)] * 3,
        compiler_params=_cparams(("parallel", "parallel")), name="hy_pre")(z, z, z, hw["conv_w"], hw["conv_b"])


def _hy_filter_kernel(ft_ref, w1_ref, b1_ref, w2_ref, b2_ref, w3_ref, b3_ref, fr_ref, dl_ref, bm_ref,
                      h_o, s_o, *, L, tl):
    i = pl.program_id(0)
    fr = fr_ref[...]
    h = jnp.sin(fr * (_dot3(ft_ref[...], w1_ref[...]) + b1_ref[...]))
    h = jnp.sin(fr * (_dot3(h, w2_ref[...]) + b2_ref[...]))
    h = _dot3(h, w3_ref[...]) + b3_ref[...]
    pos = (i * tl + lax.broadcasted_iota(I32, (tl, 1), 0)).astype(F32)
    t = pos * (1.0 / (L - 1))
    h = h * jnp.exp(-t * dl_ref[...])
    h_o[...] = h
    keep = jnp.where((pos == 0.0) & (bm_ref[...] > 0.5), 0.0, 1.0)
    part = jnp.sum(jnp.abs(h) * keep, axis=0, keepdims=True)

    @pl.when(i == 0)
    def _():
        s_o[...] = jnp.zeros_like(s_o)

    s_o[...] += part


def _hy_filter(L, hw, tl=256):
    feats = _hyena_features(L)
    ncol = hw["f_w3"].shape[1]
    fw = hw["f_w1"].shape[1]
    const = lambda i: (0, 0)
    full = lambda a: pl.BlockSpec(a.shape, const)
    args = [hw["f_w1"], hw["f_b1"], hw["f_w2"], hw["f_b2"], hw["f_w3"], hw["f_b3"], hw["f_freq"],
            hw["abs_deltas"], hw["bwd_mask"]]
    return pl.pallas_call(
        functools.partial(_hy_filter_kernel, L=L, tl=tl),
        grid=(L // tl,),
        in_specs=[pl.BlockSpec((tl, feats.shape[1]), lambda i: (i, 0))] + [full(a) for a in args],
        out_specs=[pl.BlockSpec((tl, ncol), lambda i: (i, 0)), pl.BlockSpec((1, ncol), const)],
        out_shape=[SDS((L, ncol), F32), SDS((1, ncol), F32)],
        compiler_params=_cparams(("arbitrary",)), name="hy_filter")(feats, *args)


def _hyena_features(L):
    emb_bands = 16
    t = jnp.linspace(0.0, 1.0, L, dtype=F32)[:, None]
    w = (2.0 * math.pi / L) * jnp.arange(L, dtype=F32)[:, None]
    f = jnp.linspace(1e-4, emb_bands - 1, emb_bands, dtype=F32)[None, :]
    feats = jnp.concatenate([t, jnp.cos(f * w), -jnp.sin(f * w)], axis=-1)
    return jnp.pad(feats, ((0, 0), (0, V7X_LANES - feats.shape[1])))


def _dft_tables(N):
    n2 = DFT_INNER
    n1 = N // n2
    two_pi = 2.0 * math.pi

    def cs(num, den):
        ang = (num % den).astype(F32) * (two_pi / den)
        return jnp.cos(ang), jnp.sin(ang)

    f1 = jnp.arange(n1, dtype=I32)
    c1, s1 = cs(f1[:, None] * f1[None, :], n1)
    fwd_full = jnp.concatenate([jnp.concatenate([c1, s1], 1), jnp.concatenate([-s1, c1], 1)], 0)
    h = n1 // 2
    fwd_half = jnp.concatenate([jnp.concatenate([c1[:, :h], s1[:, :h]], 1),
                                jnp.concatenate([-s1[:, :h], c1[:, :h]], 1)], 0)
    ci, si = c1[:h, :], s1[:h, :]
    inv_half = jnp.concatenate([jnp.concatenate([ci, -si], 1), jnp.concatenate([si, ci], 1)], 0) * (1.0 / N)
    f2 = jnp.arange(n2, dtype=I32)
    freq = f1[:, None, None] + n1 * f2[None, :, None]
    c2, s2 = cs(freq * f2[None, None, :], N)
    b_fwd = jnp.concatenate([jnp.concatenate([c2, s2], 2), jnp.concatenate([-s2, c2], 2)], 1)
    b_inv = jnp.swapaxes(b_fwd, 1, 2)
    return dict(n1=n1, n2=n2, fwd_full=fwd_full, fwd_half=fwd_half, inv_half=inv_half, b_fwd=b_fwd, b_inv=b_inv)


def _lead_kernel(*refs, has_inv, has_fwd, has_scale, precise):
    refs = list(refs)
    mm = _dot3 if precise else _bdot
    if has_inv:
        yp_ref, inv_ref, xga_ref, xgb_ref, ua_ref, ub_ref, bias_ref = refs[:7]
        refs = refs[7:]
        yp = yp_ref[...]
        ystk = yp.reshape(yp.shape[0] * yp.shape[1], yp.shape[2])
        y = mm(inv_ref[...], ystk)
        k = y.shape[0] // 2
        bias = bias_ref[...]
        za = xga_ref[...] * (y[:k] + bias * ua_ref[...])
        zb = xgb_ref[...] * (y[k:] + bias * ub_ref[...])
    else:
        xa_ref, xb_ref = refs[:2]
        refs = refs[2:]
        za, zb = xa_ref[...], xb_ref[...]
        if has_scale:
            sc = refs[0][...]
            refs = refs[1:]
            za, zb = za * sc, zb * sc
    if has_fwd:
        fwd_ref = refs[0]
        refs = refs[1:]
    if has_inv:
        refs[0][...] = za
        refs[1][...] = zb
        refs = refs[2:]
    if has_fwd:
        y_o = refs[0]
        yy = mm(fwd_ref[...], jnp.concatenate([za, zb], axis=0))
        y_o[...] = yy.reshape(y_o.shape).astype(y_o.dtype)


def _lead_call(*, pairs, n1, kin, lanes, wl, inv=None, fwd=None, xa=None, xb_off=None, scale=None,
               gate=None, precise=False, name="hy_lead"):
    nj = lanes // wl
    in_specs, args = [], []
    out_specs, out_shape = [], []
    has_inv = gate is not None
    if has_inv:
        kh = gate["xg"].shape[1]
        off = gate["off"]
        sa = pl.BlockSpec((None, kh, wl), lambda p, j: (p, 0, j))
        sb = pl.BlockSpec((None, kh, wl), lambda p, j: (p + off, 0, j))
        in_specs += [pl.BlockSpec((None, 2, n1, wl), lambda p, j: (p, 0, 0, j)),
                     pl.BlockSpec(inv.shape, lambda p, j: (0, 0)), sa, sb, sa, sb,
                     pl.BlockSpec((1, wl), lambda p, j: (0, j))]
        args += [gate["yp"], inv, gate["xg"], gate["xg"], gate["u"], gate["u"], gate["bias"]]
    else:
        in_specs += [pl.BlockSpec((None, kin, wl), lambda p, j: (p, 0, j)),
                     pl.BlockSpec((None, kin, wl), lambda p, j: (p + xb_off, 0, j))]
        args += [xa, xa]
        if scale is not None:
            in_specs.append(pl.BlockSpec((1, wl), lambda p, j: (0, j)))
            args.append(scale)
    if fwd is not None:
        in_specs.append(pl.BlockSpec(fwd.shape, lambda p, j: (0, 0)))
        args.append(fwd)
    if has_inv:
        out_specs += [sa, sa]
        out_shape += [SDS((pairs, kh, lanes), F32)] * 2
    if fwd is not None:
        out_specs.append(pl.BlockSpec((None, 2, n1, wl), lambda p, j: (p, 0, 0, j)))
        out_shape.append(SDS((pairs, 2, n1, lanes), F32 if precise else BF16))
    return pl.pallas_call(
        functools.partial(_lead_kernel, has_inv=has_inv, has_fwd=fwd is not None,
                          has_scale=scale is not None, precise=precise),
        grid=(pairs, nj), in_specs=in_specs, out_specs=out_specs, out_shape=out_shape,
        compiler_params=_cparams(("parallel", "parallel")), name=name)(*args)


def _inner_kernel(y_ref, bf_ref, *rest, f1t, n2, has_inv, precise):
    mm = _dot3 if precise else _bdot
    if has_inv:
        k_ref, bi_ref, o_ref = rest
    else:
        (o_ref,) = rest
    for i in range(f1t):
        ys = jnp.concatenate([y_ref[0, i], y_ref[1, i]], axis=0)
        x = mm(bf_ref[i], ys)
        xr, xi = x[:n2], x[n2:]
        if has_inv:
            kr, ki = k_ref[0, i], k_ref[1, i]
            zr = xr * kr - xi * ki
            zi = xr * ki + xi * kr
            yp = mm(bi_ref[i], jnp.concatenate([zr, zi], axis=0))
            o_ref[0, i] = yp[:n2].astype(o_ref.dtype)
            o_ref[1, i] = yp[n2:].astype(o_ref.dtype)
        else:
            o_ref[0, i] = xr
            o_ref[1, i] = xi


def _inner_call(y5, b_fwd, b_inv=None, kf=None, kblk=0, precise=False, f1t=8, name="hy_inner"):
    P, _, n1, n2, ch = y5.shape
    has_inv = kf is not None
    ys = pl.BlockSpec((None, 2, f1t, n2, ch), lambda j, p: (p, 0, j, 0, 0))
    ts = pl.BlockSpec((f1t, 2 * n2, 2 * n2), lambda j, p: (j, 0, 0))
    in_specs = [ys, ts]
    args = [y5, b_fwd]
    if has_inv:
        in_specs += [pl.BlockSpec((2, f1t, n2, ch), lambda j, p: (0, j, 0, kblk)), ts]
        args += [kf, b_inv]
    return pl.pallas_call(
        functools.partial(_inner_kernel, f1t=f1t, n2=n2, has_inv=has_inv, precise=precise),
        grid=(n1 // f1t, P), in_specs=in_specs, out_specs=ys, out_shape=SDS(y5.shape, y5.dtype),
        compiler_params=_cparams(("parallel", "parallel")), name=name)(*args)


def _hy_spectrum(L, hw, tabs):
    N = 2 * L
    n1, n2 = tabs["n1"], tabs["n2"]
    dh = hw["dh"]
    hwin, asum = _hy_filter(L, hw)
    ks, ss = [], []
    for o in range(2):
        kf = hwin[:, (2 * o) * dh:(2 * o + 1) * dh]
        kb = hwin[:, (2 * o + 1) * dh:(2 * o + 2) * dh]
        ks.append(jnp.concatenate([kf, jnp.zeros((1, dh), F32), kb[:0:-1]], axis=0))
        ss.append(asum[:, (2 * o) * dh:(2 * o + 1) * dh] + asum[:, (2 * o + 1) * dh:(2 * o + 2) * dh])
    k2 = jnp.concatenate(ks, axis=1)
    inv_mass = 1.0 / jnp.concatenate(ss, axis=1)
    lanes = n2 * 2 * dh
    kin = jnp.stack([k2.reshape(n1, lanes), jnp.zeros((n1, lanes), F32)])
    (y,) = _lead_call(pairs=1, n1=n1, kin=n1, lanes=lanes, wl=2048, fwd=tabs["fwd_full"], xa=kin, xb_off=1,
                      scale=jnp.tile(inv_mass, (1, n2)), precise=True, name="hy_filter_lead")
    kf = _inner_call(y.reshape(1, 2, n1, n2, 2 * dh), tabs["b_fwd"], precise=True, name="hy_filter_inner")
    return kf[0]


def _hyena_mixer(z, B, L, hw, tabs, kf):
    dh = hw["dh"]
    n1, n2 = tabs["n1"], tabs["n2"]
    kh = n1 // 2
    lanes = n2 * dh
    P = B // 2
    wl = 2048
    v, x1, x2 = _hy_pre(z, B, L, hw)
    view = lambda a: a.reshape(B, kh, lanes)
    v3, x13, x23 = view(v), view(x1), view(x2)
    bias = [jnp.tile(hw["bias"][o:o + 1], (1, n2)) for o in range(2)]
    (y,) = _lead_call(pairs=P, n1=n1, kin=kh, lanes=lanes, wl=wl, fwd=tabs["fwd_half"], xa=v3, xb_off=P,
                      name="hy_lead0")
    yp = _inner_call(y.reshape(P, 2, n1, n2, dh), tabs["b_fwd"], tabs["b_inv"], kf, kblk=0, name="hy_inner0")
    z1a, z1b, y = _lead_call(pairs=P, n1=n1, kin=kh, lanes=lanes, wl=wl, inv=tabs["inv_half"], fwd=tabs["fwd_half"],
                             gate=dict(yp=yp.reshape(P, 2, n1, lanes), xg=x13, u=v3, bias=bias[0], off=P),
                             name="hy_lead1")
    z1 = jnp.concatenate([z1a, z1b], axis=0)
    yp = _inner_call(y.reshape(P, 2, n1, n2, dh), tabs["b_fwd"], tabs["b_inv"], kf, kblk=1, name="hy_inner1")
    ya, yb = _lead_call(pairs=P, n1=n1, kin=kh, lanes=lanes, wl=wl, inv=tabs["inv_half"],
                        gate=dict(yp=yp.reshape(P, 2, n1, lanes), xg=x23, u=z1, bias=bias[1], off=P),
                        name="hy_lead2")
    return jnp.concatenate([ya, yb], axis=0).reshape(B * L, dh)


def _outproj_kernel(yr_ref, ys_ref, yh_ref, x_ref, wo_ref, gh_ref, gf_ref, wr_ref, xo_ref, xn_ref, aff_ref,
                    *, dr, ds):
    yh = _rms(yh_ref[...], gh_ref[...])
    acc = _bdot(yr_ref[...], wo_ref[0:dr, :])
    acc += _bdot(ys_ref[...], wo_ref[dr:dr + ds, :])
    acc += _bdot(yh, wo_ref[dr + ds:, :])
    xnew = x_ref[...] + acc
    xo_ref[...] = xnew
    xn = _rms(xnew, gf_ref[...])
    xn_ref[...] = xn
    logits = _dot3_t(wr_ref[...], xn)
    m = jnp.max(logits, axis=0, keepdims=True)
    e = jnp.exp(logits - m)
    aff_ref[...] = e / jnp.sum(e, axis=0, keepdims=True)


def _out_proj(yr, ys, yh, x, lw, tm=512):
    T, D = x.shape
    dr, ds, dh = yr.shape[1], ys.shape[1], yh.shape[1]
    E = lw["w_router_t"].shape[0]
    row = lambda i: (i, 0)
    const = lambda i: (0, 0)
    return pl.pallas_call(
        functools.partial(_outproj_kernel, dr=dr, ds=ds),
        grid=(T // tm,),
        in_specs=[pl.BlockSpec((tm, dr), row), pl.BlockSpec((tm, ds), row), pl.BlockSpec((tm, dh), row),
                  pl.BlockSpec((tm, D), row), pl.BlockSpec((D, D), const), pl.BlockSpec((1, dh), const),
                  pl.BlockSpec((1, D), const), pl.BlockSpec((E, D), const)],
        out_specs=[pl.BlockSpec((tm, D), row), pl.BlockSpec((tm, D), row), pl.BlockSpec((E, tm), lambda i: (0, i))],
        out_shape=[SDS((T, D), F32), SDS((T, D), F32), SDS((E, T), F32)],
        compiler_params=_cparams(("parallel",)), name="out_proj")(
            yr, ys, yh, x, lw["w_out"], lw["hy"]["norm_g"], lw["norm_ffn_g"], lw["w_router_t"])


def _select_prefix_kernel(aff_ref, incl_ref, off_ref, *, cap):
    aff = aff_ref[...]
    nt = aff.shape[0]
    bits = pltpu.bitcast(aff, I32)

    def body(i, thr):
        cand = thr | jnp.left_shift(jnp.int32(1), 30 - i)
        cnt = jnp.sum(jnp.where(bits >= cand, 1.0, 0.0))
        return jnp.where(cnt >= cap, cand, thr)

    thr = lax.fori_loop(0, 31, body, jnp.int32(0))
    gt = bits > thr
    eq = bits == thr
    need = cap - jnp.sum(jnp.where(gt, 1.0, 0.0))

    li = lax.broadcasted_iota(I32, (V7X_LANES, V7X_LANES), 0)
    mi = lax.broadcasted_iota(I32, (V7X_LANES, V7X_LANES), 1)
    upper = jnp.where(li <= mi, 1.0, 0.0).astype(BF16)
    ones = jnp.ones((V7X_LANES, V7X_LANES), BF16)
    ri = lax.broadcasted_iota(I32, (nt, nt), 0)
    ci = lax.broadcasted_iota(I32, (nt, nt), 1)
    lower = jnp.where(ci < ri, 1.0, 0.0).astype(BF16)
    dot = functools.partial(jnp.dot, preferred_element_type=F32)

    def prefix(m):
        mb = m.astype(BF16)
        incl = dot(mb, upper)
        tot = dot(mb, ones)
        return incl, dot(lower, tot.astype(BF16))

    eqf = jnp.where(eq, 1.0, 0.0)
    incl_eq, off_eq = prefix(eqf)
    rank_eq = off_eq + incl_eq - eqf
    sel = jnp.where(gt, 1.0, jnp.where(eq & (rank_eq < need), 1.0, 0.0))
    incl, off = prefix(sel)
    incl_ref[...] = incl
    off_ref[...] = off


def _select_index_kernel(incl_ref, off_ref, aff_ref, idx_ref, gate_ref, *, sb):
    j0 = pl.program_id(1) * sb
    incl = incl_ref[...]
    off = off_ref[...]
    nt = incl.shape[0]
    off_col = off[:, 0:1]
    end_col = off_col + incl[:, V7X_LANES - 1:V7X_LANES]
    j = (j0 + lax.broadcasted_iota(I32, (1, sb), 1)).astype(F32)
    tile_id = jnp.sum(jnp.where(end_col <= j, 1.0, 0.0), axis=0, keepdims=True)
    ti = lax.broadcasted_iota(I32, (nt, sb), 0).astype(F32)
    onehot = jnp.where(ti == tile_id, 1.0, 0.0)
    excl = jnp.sum(onehot * off_col, axis=0, keepdims=True)
    rj = j - excl
    ohb = onehot.astype(BF16)
    dot = functools.partial(jnp.dot, preferred_element_type=F32)
    rows = dot(incl.T.astype(BF16), ohb)
    local = jnp.sum(jnp.where(rows <= rj, 1.0, 0.0), axis=0, keepdims=True)
    idx_ref[...] = (tile_id * V7X_LANES + local).astype(I32)
    ah, am, al = _split3(aff_ref[...].T)
    arows = dot(ah, ohb) + dot(am, ohb) + dot(al, ohb)
    lane = lax.broadcasted_iota(I32, (V7X_LANES, sb), 0).astype(F32)
    gate_ref[...] = jnp.sum(jnp.where(lane == local, arows, 0.0), axis=0, keepdims=True)


def _moe_select(aff_t, cap, sb=1024):
    E, T = aff_t.shape
    nt = T // V7X_LANES
    aff3 = aff_t.reshape(E, nt, V7X_LANES)
    s3 = pl.BlockSpec((None, nt, V7X_LANES), lambda e: (e, 0, 0))
    incl, off = pl.pallas_call(
        functools.partial(_select_prefix_kernel, cap=cap),
        grid=(E,), in_specs=[s3], out_specs=[s3, s3],
        out_shape=[SDS((E, nt, V7X_LANES), F32)] * 2,
        compiler_params=_cparams(("parallel",)), name="moe_select_prefix")(aff3)
    s3b = pl.BlockSpec((None, nt, V7X_LANES), lambda e, j: (e, 0, 0))
    so = pl.BlockSpec((None, 1, sb), lambda e, j: (e, 0, j))
    idx, gate = pl.pallas_call(
        functools.partial(_select_index_kernel, sb=sb),
        grid=(E, cap // sb), in_specs=[s3b, s3b, s3b], out_specs=[so, so],
        out_shape=[SDS((E, 1, cap), I32), SDS((E, 1, cap), F32)],
        compiler_params=_cparams(("parallel", "parallel")), name="moe_select_index")(incl, off, aff3)
    return idx, gate


def _moe_ffn_kernel(idx_hbm, gate_ref, xn_hbm, acc_in_hbm, wg_ref, wu_ref, wd_ref, acc_hbm,
                    idx_s, xbuf, abuf, sem_i, sem_x, sem_a, sem_s, *, m, nt):
    del acc_in_hbm
    hm = m // 2
    cp = pltpu.make_async_copy(idx_hbm.at[pl.program_id(0) * nt + pl.program_id(1)], idx_s, sem_i)
    cp.start()
    cp.wait()

    def rows(k):
        return pl.ds(k * hm, hm)

    for k in range(2):
        def issue(r, carry, k=k):
            t = idx_s[k * hm + r]
            pltpu.make_async_copy(xn_hbm.at[t], xbuf.at[k * hm + r], sem_x.at[k]).start()
            pltpu.make_async_copy(acc_hbm.at[t], abuf.at[k * hm + r], sem_a.at[k]).start()
            return carry

        lax.fori_loop(0, hm, issue, 0, unroll=8)

    g_col = jnp.broadcast_to(gate_ref[...], (V7X_LANES, m)).T[:, 0:1]
    for k in range(2):
        pltpu.make_async_copy(xn_hbm.at[pl.ds(0, hm)], xbuf.at[rows(k)], sem_x.at[k]).wait()
        x = xbuf[rows(k), :].astype(BF16)
        hg = jnp.dot(x, wg_ref[...], preferred_element_type=F32)
        hu = jnp.dot(x, wu_ref[...], preferred_element_type=F32)
        h = hg * _sigmoid(hg) * hu
        y = jnp.dot(h.astype(BF16), wd_ref[...], preferred_element_type=F32)
        pltpu.make_async_copy(acc_hbm.at[pl.ds(0, hm)], abuf.at[rows(k)], sem_a.at[k]).wait()
        abuf[rows(k), :] = abuf[rows(k), :] + y * g_col[k * hm:(k + 1) * hm]

        def scatter(r, carry, k=k):
            pltpu.make_async_copy(abuf.at[k * hm + r], acc_hbm.at[idx_s[k * hm + r]], sem_s.at[k]).start()
            return carry

        lax.fori_loop(0, hm, scatter, 0, unroll=8)

    for k in range(2):
        pltpu.make_async_copy(abuf.at[rows(k)], acc_hbm.at[pl.ds(0, hm)], sem_s.at[k]).wait()


def _moe_ffn(idx, gate, xn, acc, mw, m=1024):
    T, D = xn.shape
    E, _, cap = idx.shape
    nt = cap // m
    F = mw["w_gate"].shape[2]
    idx2 = idx.reshape(E * nt, m)
    gate3 = gate.reshape(E * nt, 1, m)
    wspec = lambda a, b: pl.BlockSpec((None, a, b), lambda e, i: (e, 0, 0))
    any_spec = pl.BlockSpec(memory_space=pl.ANY)
    return pl.pallas_call(
        functools.partial(_moe_ffn_kernel, m=m, nt=nt),
        grid=(E, nt),
        in_specs=[any_spec, pl.BlockSpec((None, 1, m), lambda e, i: (e * nt + i, 0, 0)), any_spec, any_spec,
                  wspec(D, F), wspec(D, F), wspec(F, D)],
        out_specs=any_spec,
        out_shape=SDS((T, D), F32),
        input_output_aliases={3: 0},
        scratch_shapes=[pltpu.SMEM((m,), I32), pltpu.VMEM((m, D), F32), pltpu.VMEM((m, D), F32),
                        pltpu.SemaphoreType.DMA, pltpu.SemaphoreType.DMA((2,)), pltpu.SemaphoreType.DMA((2,)),
                        pltpu.SemaphoreType.DMA((2,))],
        compiler_params=_cparams(("arbitrary", "arbitrary")), name="moe_ffn")(
            idx2, gate3, xn, acc, mw["w_gate"], mw["w_up"], mw["w_down"])


def _moe(xnew, xn, aff_t, mw):
    T = xn.shape[0]
    E = aff_t.shape[0]
    cap = EC_CAPACITY_FACTOR * T // E
    idx, gate = _moe_select(aff_t, cap)
    return _moe_ffn(idx, gate, xn, xnew, mw)


def _norm_kernel(x_ref, g_ref, o_ref):
    o_ref[...] = _rms(x_ref[...], g_ref[...])


def _final_norm(x, g, tm=512):
    T, D = x.shape
    return pl.pallas_call(
        _norm_kernel, grid=(T // tm,),
        in_specs=[pl.BlockSpec((tm, D), lambda i: (i, 0)), pl.BlockSpec((1, D), lambda i: (0, 0))],
        out_specs=pl.BlockSpec((tm, D), lambda i: (i, 0)), out_shape=SDS((T, D), F32),
        compiler_params=_cparams(("parallel",)), name="final_norm")(x, g)


def _block_diag_rows(blocks):
    n, r, c = blocks.shape
    eye = jnp.eye(n, dtype=blocks.dtype)
    return jnp.einsum("nrc,nm->nrmc", blocks, eye).reshape(n * r, n * c)


def _rwkv_weights(i, p):
    heads, hd = p["rwkv_r_k"].shape[1:]
    dr = heads * hd
    row = lambda a: a.reshape(1, -1)
    head_id = jnp.arange(dr) // hd
    return dict(
        dr=dr, heads=heads, hd=hd,
        mu_prev=row(p["rwkv_mu_prev"][i]), mu_next=row(p["rwkv_mu_next"][i]),
        w0=row(p["rwkv_w0"][i]), wup=_block_diag_rows(p["rwkv_w_up"][i]).astype(BF16),
        a0=row(p["rwkv_a0"][i]), aup=_block_diag_rows(p["rwkv_a_up"][i]).astype(BF16),
        gup=p["rwkv_g_up"][i].astype(BF16),
        k_k=row(p["rwkv_k_k"][i]), k_a=row(p["rwkv_k_a"][i]), r_k=row(p["rwkv_r_k"][i]),
        ln_w=row(p["rwkv_ln_w"][i]), ln_b=row(p["rwkv_ln_b"][i]),
        hsum=(head_id[:, None] == head_id[None, :]).astype(BF16))


def _s5_weights(i, p):
    lam_re, lam_im, log_dt = p["s5_lam_re"][i], p["s5_lam_im"][i], p["s5_log_dt"][i]
    b_re, b_im, c_re, c_im = p["s5_b_re"][i], p["s5_b_im"][i], p["s5_c_re"][i], p["s5_c_im"][i]
    G, P, H = b_re.shape
    l_re = jnp.minimum(lam_re, -1e-4)
    dt = jnp.exp(log_dt)[..., None]
    mag = jnp.exp(l_re * dt)
    ab_re = mag * jnp.cos(lam_im * dt)
    ab_im = mag * jnp.sin(lam_im * dt)
    den = l_re * l_re + lam_im * lam_im
    n_re = ab_re - 1.0
    f_re = (n_re * l_re + ab_im * lam_im) / den
    f_im = (ab_im * l_re - n_re * lam_im) / den
    w_re = jnp.swapaxes(f_re[..., None] * b_re - f_im[..., None] * b_im, 2, 3)
    w_im = jnp.swapaxes(f_re[..., None] * b_im + f_im[..., None] * b_re, 2, 3)
    win = jnp.stack([jnp.concatenate([_block_diag_rows(w_re[d]), _block_diag_rows(w_im[d])], axis=1)
                     for d in range(2)]).astype(BF16)
    cout = jnp.concatenate([_block_diag_rows(jnp.swapaxes(c_re, 1, 2)),
                            _block_diag_rows(jnp.swapaxes(-c_im, 1, 2))], axis=0).astype(BF16)
    ns = G * P
    return dict(ns=ns, cu=G * H, win=win, cout=cout,
                a_re=ab_re.reshape(2, 1, ns), a_im=ab_im.reshape(2, 1, ns),
                d=p["s5_d"][i].reshape(1, -1), w_glu=p["s5_w_glu"][i].astype(BF16),
                norm_g=p["s5_norm_g"][i].reshape(1, -1))


def _hyena_weights(i, p):
    dh = p["hy_norm_g"].shape[1]
    emb = p["hy_f_w1"].shape[1]
    w1 = jnp.pad(p["hy_f_w1"][i], ((0, V7X_LANES - emb), (0, 0)))
    deltas = jnp.linspace(math.log(HYENA_TARGET) / HYENA_FAST_DECAY, math.log(HYENA_TARGET) / HYENA_SLOW_DECAY,
                          dh, dtype=F32)
    ncol = p["hy_f_w3"].shape[2]
    col_dir = (jnp.arange(ncol) // dh) % 2
    row = lambda a: a.reshape(1, -1)
    return dict(dh=dh, conv_w=p["hy_conv_w"][i], conv_b=row(p["hy_conv_b"][i]),
                f_w1=w1, f_b1=row(p["hy_f_b1"][i]), f_w2=p["hy_f_w2"][i], f_b2=row(p["hy_f_b2"][i]),
                f_w3=p["hy_f_w3"][i], f_b3=row(p["hy_f_b3"][i]), f_freq=row(p["hy_f_freq"][i]),
                abs_deltas=jnp.tile(jnp.abs(deltas), ncol // dh).reshape(1, -1),
                bwd_mask=col_dir.astype(F32).reshape(1, -1),
                bias=p["hy_bias"][i], norm_g=row(p["hy_norm_g"][i]))


def _layer_weights(i, p):
    return dict(
        norm_mix_g=p["norm_mix_g"][i].reshape(1, -1), w_in=p["w_in"][i].astype(BF16),
        rwkv=_rwkv_weights(i, p), s5=_s5_weights(i, p), hy=_hyena_weights(i, p),
        w_out=p["w_out"][i].astype(BF16), norm_ffn_g=p["norm_ffn_g"][i].reshape(1, -1),
        w_router_t=p["moe_w_router"][i].T,
        moe=dict(w_gate=p["moe_w_gate"][i].astype(BF16), w_up=p["moe_w_up"][i].astype(BF16),
                 w_down=p["moe_w_down"][i].astype(BF16)))


def _layer(x, B, L, lw, tabs, kf):
    c_r = lw["rwkv"]["mu_prev"].shape[1]
    c_s = lw["s5"]["cu"]
    c_h = 3 * lw["hy"]["dh"]
    z_r, z_s, z_h = _in_proj(x, B, L, lw["norm_mix_g"], lw["w_in"], c_r, c_s, c_h)
    y_r = _rwkv_mixer(z_r, B, L, lw["rwkv"])
    y_s = _s5_mixer(z_s, B, L, lw["s5"])
    y_h = _hyena_mixer(z_h, B, L, lw["hy"], tabs, kf)
    xnew, xn, aff_t = _out_proj(y_r, y_s, y_h, x, lw)
    return _moe(xnew, xn, aff_t, lw["moe"])


def _trunk(x, layers, final_g):
    B, L, D = x.shape
    tabs = _dft_tables(2 * L)
    h = x.reshape(B * L, D)
    for lw in layers:
        kf = _hy_spectrum(L, lw["hy"], tabs)
        h = _layer(h, B, L, lw, tabs, kf)
    return _final_norm(h, final_g.reshape(1, -1)).reshape(B, L, D)


def kernel(x_prompt, x_sample, norm_mix_g, w_in, rwkv_mu_prev, rwkv_mu_next, rwkv_w0, rwkv_w_up, rwkv_a0, rwkv_a_up, rwkv_g_up, rwkv_k_k, rwkv_k_a, rwkv_r_k, rwkv_ln_w, rwkv_ln_b, s5_lam_re, s5_lam_im, s5_log_dt, s5_b_re, s5_b_im, s5_c_re, s5_c_im, s5_d, s5_w_glu, s5_norm_g, hy_conv_w, hy_conv_b, hy_f_w1, hy_f_b1, hy_f_w2, hy_f_b2, hy_f_w3, hy_f_b3, hy_f_freq, hy_bias, hy_norm_g, w_out, norm_ffn_g, moe_w_router, moe_w_gate, moe_w_up, moe_w_down, final_norm_g):
    p = dict(norm_mix_g=norm_mix_g, w_in=w_in, rwkv_mu_prev=rwkv_mu_prev, rwkv_mu_next=rwkv_mu_next,
             rwkv_w0=rwkv_w0, rwkv_w_up=rwkv_w_up, rwkv_a0=rwkv_a0, rwkv_a_up=rwkv_a_up,
             rwkv_g_up=rwkv_g_up, rwkv_k_k=rwkv_k_k, rwkv_k_a=rwkv_k_a, rwkv_r_k=rwkv_r_k,
             rwkv_ln_w=rwkv_ln_w, rwkv_ln_b=rwkv_ln_b, s5_lam_re=s5_lam_re, s5_lam_im=s5_lam_im,
             s5_log_dt=s5_log_dt, s5_b_re=s5_b_re, s5_b_im=s5_b_im, s5_c_re=s5_c_re, s5_c_im=s5_c_im,
             s5_d=s5_d, s5_w_glu=s5_w_glu, s5_norm_g=s5_norm_g, hy_conv_w=hy_conv_w, hy_conv_b=hy_conv_b,
             hy_f_w1=hy_f_w1, hy_f_b1=hy_f_b1, hy_f_w2=hy_f_w2, hy_f_b2=hy_f_b2, hy_f_w3=hy_f_w3,
             hy_f_b3=hy_f_b3, hy_f_freq=hy_f_freq, hy_bias=hy_bias, hy_norm_g=hy_norm_g, w_out=w_out,
             norm_ffn_g=norm_ffn_g, moe_w_router=moe_w_router, moe_w_gate=moe_w_gate,
             moe_w_up=moe_w_up, moe_w_down=moe_w_down)
    layers = [_layer_weights(i, p) for i in range(w_in.shape[0])]
    return (_trunk(x_prompt, layers, final_norm_g), _trunk(x_sample, layers, final_norm_g))
```

```python
import functools
import math

import jax
import jax.numpy as jnp
from jax import lax
from jax.experimental import pallas as pl
from jax.experimental.pallas import tpu as pltpu

F32 = jnp.float32
BF16 = jnp.bfloat16
I32 = jnp.int32
SDS = jax.ShapeDtypeStruct

EPS_NORM = 1e-6
EPS_GN = 64e-5
HYENA_FAST_DECAY = 0.3
HYENA_SLOW_DECAY = 1.5
HYENA_TARGET = 1e-2
EC_CAPACITY_FACTOR = 2

V7X_LANES = 128
V7X_SUBLANES = 8
V7X_VMEM_LIMIT_BYTES = 56 * 1024 * 1024

RWKV_CHUNK = 64
DFT_INNER = 64


def _cparams(sem, vmem=V7X_VMEM_LIMIT_BYTES):
    return pltpu.CompilerParams(dimension_semantics=sem, vmem_limit_bytes=vmem)


def _bdot(a, b):
    return jnp.dot(a.astype(BF16), b.astype(BF16), preferred_element_type=F32)


def _bdot_t(a, b):
    return lax.dot_general(a.astype(BF16), b.astype(BF16), (((1,), (1,)), ((), ())),
                           preferred_element_type=F32)


def _split2(x):
    hi = x.astype(BF16)
    lo = (x - hi.astype(F32)).astype(BF16)
    return hi, lo


def _split3(x):
    hi = x.astype(BF16)
    r1 = x - hi.astype(F32)
    mid = r1.astype(BF16)
    lo = (r1 - mid.astype(F32)).astype(BF16)
    return hi, mid, lo


def _dot3(a, b):
    ah, al = _split2(a)
    bh, bl = _split2(b)
    d = functools.partial(jnp.dot, preferred_element_type=F32)
    return d(ah, bh) + d(ah, bl) + d(al, bh)


def _dot3_t(a, b):
    ah, al = _split2(a)
    bh, bl = _split2(b)
    d = lambda x, y: lax.dot_general(x, y, (((1,), (1,)), ((), ())), preferred_element_type=F32)
    return d(ah, bh) + d(ah, bl) + d(al, bh)


def _dot_exact_rhs(m01, x):
    xh, xl = _split2(x)
    d = functools.partial(jnp.dot, preferred_element_type=F32)
    return d(m01, xh) + d(m01, xl)


def _dot_exact_lhs(x, m01):
    xh, xl = _split2(x)
    d = functools.partial(jnp.dot, preferred_element_type=F32)
    return d(xh, m01) + d(xl, m01)


def _rms(x, g):
    return x * lax.rsqrt(jnp.mean(x * x, axis=-1, keepdims=True) + EPS_NORM) * g


def _sigmoid(x):
    return 1.0 / (1.0 + jnp.exp(-x))


def _softplus(x):
    return jnp.maximum(x, 0.0) + jnp.log(1.0 + jnp.exp(-jnp.abs(x)))


def _shift_rows(z, prev_row, next_row):
    n = z.shape[0]
    rows = lax.broadcasted_iota(I32, z.shape, 0)
    zprev = jnp.where(rows == 0, prev_row, pltpu.roll(z, 1, 0))
    znext = jnp.where(rows == n - 1, next_row, pltpu.roll(z, n - 1, 0))
    return zprev, znext


def _halo_specs(tm, cols, nl, total_rows):
    r8 = tm // V7X_SUBLANES
    last = total_rows // V7X_SUBLANES - 1
    prev = pl.BlockSpec((V7X_SUBLANES, cols), lambda b, l: (jnp.maximum((b * nl + l) * r8 - 1, 0), 0))
    nxt = pl.BlockSpec((V7X_SUBLANES, cols), lambda b, l: (jnp.minimum((b * nl + l + 1) * r8, last), 0))
    return prev, nxt


def _halo_rows(zp_ref, zn_ref):
    l = pl.program_id(1)
    nl = pl.num_programs(1)
    prev_row = jnp.where(l == 0, 0.0, zp_ref[V7X_SUBLANES - 1:V7X_SUBLANES, :])
    next_row = jnp.where(l == nl - 1, 0.0, zn_ref[0:1, :])
    return prev_row, next_row


def _inproj_kernel(x_ref, g_ref, w_ref, zr_ref, zs_ref, zh_ref, *, c_r, c_s):
    h = _rms(x_ref[...], g_ref[...])
    z = _bdot(h, w_ref[...])
    zr_ref[...] = z[:, :c_r]
    zs_ref[...] = z[:, c_r:c_r + c_s]
    zh_ref[...] = z[:, c_r + c_s:]


def _in_proj(x, B, L, g, w, c_r, c_s, c_h, tm=512):
    T, D = x.shape
    nl = L // tm
    return pl.pallas_call(
        functools.partial(_inproj_kernel, c_r=c_r, c_s=c_s),
        grid=(B, nl),
        in_specs=[pl.BlockSpec((tm, D), lambda b, l: (b * nl + l, 0)),
                  pl.BlockSpec((1, D), lambda b, l: (0, 0)),
                  pl.BlockSpec((D, c_r + c_s + c_h), lambda b, l: (0, 0))],
        out_specs=[pl.BlockSpec((tm, c_r), lambda b, l: (b * nl + l, 0)),
                   pl.BlockSpec((tm, c_s), lambda b, l: (l, b)),
                   pl.BlockSpec((tm, c_h), lambda b, l: (b * nl + l, 0))],
        out_shape=[SDS((T, c_r), F32), SDS((L, B * c_s), F32), SDS((T, c_h), F32)],
        compiler_params=_cparams(("parallel", "parallel")), name="in_proj")(x, g, w)


def _rwkv_pre_kernel(z_ref, zp_ref, zn_ref, mup_ref, mun_ref, w0_ref, wup_ref, a0_ref, aup_ref, gup_ref,
                     kk_ref, ka_ref, hs_ref,
                     r_o, v_o, kkn_o, g_o, ks_o, lw_o, kd_o, bd_o, *, dr):
    z = z_ref[...]
    prev_row, next_row = _halo_rows(zp_ref, zn_ref)
    zprev, znext = _shift_rows(z, prev_row, next_row)
    zs = z + mup_ref[...] * (zprev - z) + mun_ref[...] * (znext - z)
    r = zs[:, 0:dr]
    k = zs[:, dr:2 * dr]
    v = zs[:, 2 * dr:3 * dr]
    c3 = 3 * dr
    wd = zs[:, c3:c3 + 128]
    ad = zs[:, c3 + 128:c3 + 256]
    gd = zs[:, c3 + 256:c3 + 384]
    w_raw = w0_ref[...] + _bdot(jnp.tanh(wd), wup_ref[...])
    lw = -jnp.exp(-_softplus(-w_raw) - 0.5)
    a = _sigmoid(a0_ref[...] + _bdot(ad, aup_ref[...]))
    g = _bdot(_sigmoid(gd), gup_ref[...])
    kk = k * kk_ref[...]
    ss = _dot_exact_lhs(kk * kk, hs_ref[...])
    kk = kk / jnp.maximum(jnp.sqrt(ss), 1e-12)
    ka = ka_ref[...]
    r_o[...] = r
    v_o[...] = v
    kkn_o[...] = kk
    g_o[...] = g
    ksum = None
    for d in range(2):
        a_d = a[:, d * dr:(d + 1) * dr]
        k_d = k * (1.0 + (a_d - 1.0) * ka)
        lw_o[d] = lw[:, d * dr:(d + 1) * dr]
        kd_o[d] = k_d
        bd_o[d] = kk * a_d
        ksum = k_d if ksum is None else ksum + k_d
    ks_o[...] = ksum


def _rwkv_pre(z, B, L, pw, tm=256):
    T, cr = z.shape
    dr = pw["dr"]
    nl = L // tm
    row = lambda b, l: (b * nl + l, 0)
    const = lambda b, l: (0, 0)
    prev, nxt = _halo_specs(tm, cr, nl, T)
    full = lambda a: pl.BlockSpec(a.shape, const)
    params = [pw["mu_prev"], pw["mu_next"], pw["w0"], pw["wup"], pw["a0"], pw["aup"], pw["gup"],
              pw["k_k"], pw["k_a"], pw["hsum"]]
    o1 = pl.BlockSpec((tm, dr), row)
    o2 = pl.BlockSpec((2, tm, dr), lambda b, l: (0, b * nl + l, 0))
    return pl.pallas_call(
        functools.partial(_rwkv_pre_kernel, dr=dr),
        grid=(B, nl),
        in_specs=[pl.BlockSpec((tm, cr), row), prev, nxt] + [full(a) for a in params],
        out_specs=[o1] * 5 + [o2] * 3,
        out_shape=[SDS((T, dr), F32)] * 5 + [SDS((2, T, dr), F32)] * 3,
        compiler_params=_cparams(("parallel", "parallel")), name="rwkv_pre")(z, z, z, *params)


def _bmm(a, b):
    return lax.dot_general(a.astype(BF16), b.astype(BF16), (((2,), (1,)), ((0,), (0,))),
                           preferred_element_type=F32)


def _bmm_t(a, b):
    return lax.dot_general(a.astype(BF16), b.astype(BF16), (((2,), (2,)), ((0,), (0,))),
                           preferred_element_type=F32)


def _rwkv_chunk_operands(d, r_ref, v_ref, kk_ref, lw_ref, kd_ref, bd_ref, *, heads, hd):
    C = r_ref.shape[0]
    ti = lax.broadcasted_iota(I32, (C, C), 0)
    si = lax.broadcasted_iota(I32, (C, C), 1)
    dd = (si - ti) if d == 0 else (ti - si)
    lw = lw_ref[...]
    cum = _dot_exact_rhs(jnp.where(dd <= 0, 1.0, 0.0).astype(BF16), lw)
    tot = jnp.sum(lw, axis=0, keepdims=True)
    kd = kd_ref[...]
    bd = bd_ref[...]
    p_inv = jnp.exp(-cum)
    p_end = jnp.exp(tot - cum)

    def split(x):
        return jnp.stack([x[:, h * hd:(h + 1) * hd] for h in range(heads)])

    return dict(
        incl=(dd <= 0)[None], strict=(dd < 0)[None], eye=jnp.where(dd == 0, 1.0, 0.0)[None],
        xq=split(jnp.concatenate([kk_ref[...] * jnp.exp(cum - lw), r_ref[...] * jnp.exp(cum)], axis=0).astype(BF16)),
        kt=split((kd * p_inv).astype(BF16)), bt=split((bd * p_inv).astype(BF16)),
        kb=split(jnp.concatenate([kd * p_end, bd * p_end], axis=0).astype(BF16)),
        v=split(v_ref[...].astype(BF16)), p_tot=split(jnp.exp(tot)))


def _rwkv_chunk_kernel(r0, v0, kk0, lw0, kd0, bd0, r1, v1, kk1, lw1, kd1, bd1, y0, y1, s_ref, *, heads, hd):
    @pl.when(pl.program_id(1) == 0)
    def _():
        s_ref[...] = jnp.zeros_like(s_ref)

    C = r0.shape[0]
    ops = [_rwkv_chunk_operands(0, r0, v0, kk0, lw0, kd0, bd0, heads=heads, hd=hd),
           _rwkv_chunk_operands(1, r1, v1, kk1, lw1, kd1, bd1, heads=heads, hd=hd)]
    dirs = (0, 1)
    a_k = [_bmm_t(ops[d]["xq"], ops[d]["kt"]) for d in dirs]
    a_b = [_bmm_t(ops[d]["xq"], ops[d]["bt"]) for d in dirs]
    a_kq = [jnp.concatenate([jnp.where(ops[d]["strict"], a_k[d][:, :C], 0.0),
                             jnp.where(ops[d]["incl"], a_k[d][:, C:], 0.0)], axis=1) for d in dirs]
    a_rb = [jnp.where(ops[d]["incl"], a_b[d][:, C:], 0.0) for d in dirs]
    n1 = [jnp.where(ops[d]["strict"], a_b[d][:, :C], 0.0) for d in dirs]
    tinv = [ops[d]["eye"] - n1[d] for d in dirs]
    npow = [_bmm(n1[d], n1[d]) for d in dirs]
    for _ in range(int(math.log2(C)) - 2):
        both = [_bmm(jnp.concatenate([tinv[d], npow[d]], axis=1), npow[d]) for d in dirs]
        tinv = [tinv[d] + both[d][:, :C] for d in dirs]
        npow = [both[d][:, C:] for d in dirs]
    tinv = [tinv[d] + _bmm(tinv[d], npow[d]) for d in dirs]
    s_old = [s_ref[d] for d in dirs]
    xs = [_bmm_t(ops[d]["xq"], s_old[d]) for d in dirs]
    av = [_bmm(a_kq[d], ops[d]["v"]) for d in dirs]
    u = [_bmm(tinv[d], -(xs[d][:, :C] + av[d][:, :C])) for d in dirs]
    o = [xs[d][:, C:] + av[d][:, C:] + _bmm(a_rb[d], u[d]) for d in dirs]
    for d, y_ref in zip(dirs, (y0, y1)):
        vu = jnp.concatenate([ops[d]["v"].astype(F32), u[d]], axis=1)
        upd = lax.dot_general(vu.astype(BF16), ops[d]["kb"], (((1,), (1,)), ((0,), (0,))),
                              preferred_element_type=F32)
        s_ref[d] = s_old[d] * ops[d]["p_tot"] + upd
        for h in range(heads):
            y_ref[:, h * hd:(h + 1) * hd] = o[d][h]


def _rwkv_chunk(r, v, kk, lw, kd, bd, B, L, heads, hd):
    T, dr = r.shape
    C = RWKV_CHUNK
    nc = L // C
    f1 = pl.BlockSpec((C, dr), lambda b, c: (b * nc + c, 0))
    b1 = pl.BlockSpec((C, dr), lambda b, c: (b * nc + nc - 1 - c, 0))
    f2 = pl.BlockSpec((None, C, dr), lambda b, c: (0, b * nc + c, 0))
    b2 = pl.BlockSpec((None, C, dr), lambda b, c: (1, b * nc + nc - 1 - c, 0))
    return pl.pallas_call(
        functools.partial(_rwkv_chunk_kernel, heads=heads, hd=hd),
        grid=(B, nc),
        in_specs=[f1, f1, f1, f2, f2, f2, b1, b1, b1, b2, b2, b2],
        out_specs=[f1, b1],
        out_shape=[SDS((T, dr), F32)] * 2,
        scratch_shapes=[pltpu.VMEM((2, heads, hd, hd), F32)],
        compiler_params=_cparams(("parallel", "arbitrary")), name="rwkv_chunk")(
            r, v, kk, lw, kd, bd, r, v, kk, lw, kd, bd)


def _rwkv_post_kernel(y0_ref, y1_ref, r_ref, v_ref, ks_ref, g_ref, rk_ref, lnw_ref, lnb_ref, hs_ref, o_ref, *, hd):
    y = y0_ref[...] + y1_ref[...]
    hs = hs_ref[...]
    mean = _dot_exact_lhs(y, hs) * (1.0 / hd)
    yc = y - mean
    var = _dot_exact_lhs(yc * yc, hs) * (1.0 / hd)
    yn = yc * lax.rsqrt(var + EPS_GN) * lnw_ref[...] + lnb_ref[...]
    bonus = _dot_exact_lhs(r_ref[...] * ks_ref[...] * rk_ref[...], hs) * v_ref[...]
    o_ref[...] = (yn + bonus) * g_ref[...]


def _rwkv_post(y0, y1, r, v, ks, g, pw, tm=512):
    T, dr = r.shape
    row = lambda i: (i, 0)
    const = lambda i: (0, 0)
    s1 = pl.BlockSpec((tm, dr), row)
    vec = pl.BlockSpec((1, dr), const)
    return pl.pallas_call(
        functools.partial(_rwkv_post_kernel, hd=pw["hd"]),
        grid=(T // tm,),
        in_specs=[s1, s1, s1, s1, s1, s1, vec, vec, vec, pl.BlockSpec((dr, dr), const)],
        out_specs=s1,
        out_shape=SDS((T, dr), F32),
        compiler_params=_cparams(("parallel",)), name="rwkv_post")(
            y0, y1, r, v, ks, g, pw["r_k"], pw["ln_w"], pw["ln_b"], pw["hsum"])


def _rwkv_mixer(z, B, L, pw):
    r, v, kk, g, ks, lw, kd, bd = _rwkv_pre(z, B, L, pw)
    y0, y1 = _rwkv_chunk(r, v, kk, lw, kd, bd, B, L, pw["heads"], pw["hd"])
    return _rwkv_post(y0, y1, r, v, ks, g, pw)


def _s5_scan_kernel(u_ref, win_ref, are_ref, aim_ref, cout_ref, y_ref, x_s, st_s, *, ns):
    d = pl.program_id(0)
    c = pl.program_id(2)
    lc, bs, cu = u_ref.shape

    @pl.when(c == 0)
    def _():
        st_s[...] = jnp.zeros_like(st_s)

    u = u_ref[...].reshape(lc * bs, cu)
    x_s[...] = _bdot(u, win_ref[...])
    are = jnp.broadcast_to(are_ref[...], (bs, ns))
    aim = jnp.broadcast_to(aim_ref[...], (bs, ns))

    def body(i, carry):
        xr, xi = carry
        t = jnp.where(d == 0, i, lc - 1 - i)
        off = pl.multiple_of(t * bs, bs)
        ir = x_s[pl.ds(off, bs), 0:ns]
        ii = x_s[pl.ds(off, bs), ns:2 * ns]
        nr = are * xr - aim * xi + ir
        ni = are * xi + aim * xr + ii
        x_s[pl.ds(off, bs), 0:ns] = nr
        x_s[pl.ds(off, bs), ns:2 * ns] = ni
        return nr, ni

    xr, xi = lax.fori_loop(0, lc, body, (st_s[0], st_s[1]))
    st_s[0] = xr
    st_s[1] = xi
    y = _bdot(x_s[...], cout_ref[...])
    y_ref[...] = y.reshape(lc, bs, y_ref.shape[-1])


def _s5_scan(u3, sw, lc=128):
    L, B, cu = u3.shape
    bs = V7X_SUBLANES
    ns = sw["ns"]
    nl = L // lc

    def tix(d, c):
        return jnp.where(d == 0, c, nl - 1 - c)

    return pl.pallas_call(
        functools.partial(_s5_scan_kernel, ns=ns),
        grid=(2, B // bs, nl),
        in_specs=[pl.BlockSpec((lc, bs, cu), lambda d, b, c: (tix(d, c), b, 0)),
                  pl.BlockSpec((None, cu, 2 * ns), lambda d, b, c: (d, 0, 0)),
                  pl.BlockSpec((None, 1, ns), lambda d, b, c: (d, 0, 0)),
                  pl.BlockSpec((None, 1, ns), lambda d, b, c: (d, 0, 0)),
                  pl.BlockSpec((2 * ns, cu), lambda d, b, c: (0, 0))],
        out_specs=pl.BlockSpec((None, lc, bs, cu), lambda d, b, c: (d, tix(d, c), b, 0)),
        out_shape=SDS((2, L, B, cu), F32),
        scratch_shapes=[pltpu.VMEM((lc * bs, 2 * ns), F32), pltpu.VMEM((2, bs, ns), F32)],
        compiler_params=_cparams(("parallel", "parallel", "arbitrary")), name="s5_scan")(
            u3, sw["win"], sw["a_re"], sw["a_im"], sw["cout"])


def _gelu_tanh(x):
    return 0.5 * x * (1.0 + jnp.tanh(math.sqrt(2.0 / math.pi) * (x + 0.044715 * (x * x * x))))


def _s5_post_kernel(y_ref, u_ref, d_ref, wg_ref, g_ref, o_ref):
    u = u_ref[...]
    y = y_ref[0] + y_ref[1] + d_ref[...] * u
    h = _gelu_tanh(y)
    out = h * _sigmoid(_bdot(h, wg_ref[...]))
    o_ref[...] = _rms(out, g_ref[...])


def _s5_post(y2, u_tm, B, L, sw, tl=512):
    cu = sw["cu"]
    nl = L // tl
    const = lambda b, l: (0, 0)
    return pl.pallas_call(
        _s5_post_kernel,
        grid=(B, nl),
        in_specs=[pl.BlockSpec((2, tl, cu), lambda b, l: (0, l, b)),
                  pl.BlockSpec((tl, cu), lambda b, l: (l, b)),
                  pl.BlockSpec((1, cu), const), pl.BlockSpec((cu, cu), const), pl.BlockSpec((1, cu), const)],
        out_specs=pl.BlockSpec((tl, cu), lambda b, l: (b * nl + l, 0)),
        out_shape=SDS((B * L, cu), F32),
        compiler_params=_cparams(("parallel", "parallel")), name="s5_post")(
            y2.reshape(2, L, B * cu), u_tm, sw["d"], sw["w_glu"], sw["norm_g"])


def _s5_mixer(u_tm, B, L, sw):
    cu = sw["cu"]
    y2 = _s5_scan(u_tm.reshape(L, B, cu), sw)
    return _s5_post(y2, u_tm, B, L, sw)


def _hy_pre_kernel(z_ref, zp_ref, zn_ref, cw_ref, cb_ref, v_o, x1_o, x2_o, *, dh):
    z = z_ref[...]
    prev_row, next_row = _halo_rows(zp_ref, zn_ref)
    zprev, znext = _shift_rows(z, prev_row, next_row)
    zc = cw_ref[0:1, :] * zprev + cw_ref[1:2, :] * z + cw_ref[2:3, :] * znext + cb_ref[...]
    v_o[...] = zc[:, 0:dh]
    x1_o[...] = zc[:, dh:2 * dh]
    x2_o[...] = zc[:, 2 * dh:3 * dh]


def _hy_pre(z, B, L, hw, tm=512):
    T, ch = z.shape
    dh = hw["dh"]
    nl = L // tm
    row = lambda b, l: (b * nl + l, 0)
    const = lambda b, l: (0, 0)
    prev, nxt = _halo_specs(tm, ch, nl, T)
    o = pl.BlockSpec((tm, dh), row)
    return pl.pallas_call(
        functools.partial(_hy_pre_kernel, dh=dh),
        grid=(B, nl),
        in_specs=[pl.BlockSpec((tm, ch), row), prev, nxt,
                  pl.BlockSpec((3, ch), const), pl.BlockSpec((1, ch), const)],
        out_specs=[o, o, o],
        out_shape=[SDS((T, dh), F32)] * 3,
        compiler_params=_cparams(("parallel", "parallel")), name="hy_pre")(z, z, z, hw["conv_w"], hw["conv_b"])


def _hy_filter_kernel(ft_ref, w1_ref, b1_ref, w2_ref, b2_ref, w3_ref, b3_ref, fr_ref, dl_ref, bm_ref,
                      h_o, s_o, *, L, tl):
    i = pl.program_id(0)
    fr = fr_ref[...]
    h = jnp.sin(fr * (_dot3(ft_ref[...], w1_ref[...]) + b1_ref[...]))
    h = jnp.sin(fr * (_dot3(h, w2_ref[...]) + b2_ref[...]))
    h = _dot3(h, w3_ref[...]) + b3_ref[...]
    pos = (i * tl + lax.broadcasted_iota(I32, (tl, 1), 0)).astype(F32)
    t = pos * (1.0 / (L - 1))
    h = h * jnp.exp(-t * dl_ref[...])
    h_o[...] = h
    keep = jnp.where((pos == 0.0) & (bm_ref[...] > 0.5), 0.0, 1.0)
    part = jnp.sum(jnp.abs(h) * keep, axis=0, keepdims=True)

    @pl.when(i == 0)
    def _():
        s_o[...] = jnp.zeros_like(s_o)

    s_o[...] += part


def _hy_filter(L, hw, tl=256):
    feats = _hyena_features(L)
    ncol = hw["f_w3"].shape[1]
    fw = hw["f_w1"].shape[1]
    const = lambda i: (0, 0)
    full = lambda a: pl.BlockSpec(a.shape, const)
    args = [hw["f_w1"], hw["f_b1"], hw["f_w2"], hw["f_b2"], hw["f_w3"], hw["f_b3"], hw["f_freq"],
            hw["abs_deltas"], hw["bwd_mask"]]
    return pl.pallas_call(
        functools.partial(_hy_filter_kernel, L=L, tl=tl),
        grid=(L // tl,),
        in_specs=[pl.BlockSpec((tl, feats.shape[1]), lambda i: (i, 0))] + [full(a) for a in args],
        out_specs=[pl.BlockSpec((tl, ncol), lambda i: (i, 0)), pl.BlockSpec((1, ncol), const)],
        out_shape=[SDS((L, ncol), F32), SDS((1, ncol), F32)],
        compiler_params=_cparams(("arbitrary",)), name="hy_filter")(feats, *args)


def _hyena_features(L):
    emb_bands = 16
    t = jnp.linspace(0.0, 1.0, L, dtype=F32)[:, None]
    w = (2.0 * math.pi / L) * jnp.arange(L, dtype=F32)[:, None]
    f = jnp.linspace(1e-4, emb_bands - 1, emb_bands, dtype=F32)[None, :]
    feats = jnp.concatenate([t, jnp.cos(f * w), -jnp.sin(f * w)], axis=-1)
    return jnp.pad(feats, ((0, 0), (0, V7X_LANES - feats.shape[1])))


def _dft_tables(N):
    n2 = DFT_INNER
    n1 = N // n2
    two_pi = 2.0 * math.pi

    def cs(num, den):
        ang = (num % den).astype(F32) * (two_pi / den)
        return jnp.cos(ang), jnp.sin(ang)

    f1 = jnp.arange(n1, dtype=I32)
    c1, s1 = cs(f1[:, None] * f1[None, :], n1)
    fwd_full = jnp.concatenate([jnp.concatenate([c1, s1], 1), jnp.concatenate([-s1, c1], 1)], 0)
    h = n1 // 2
    fwd_half = jnp.concatenate([jnp.concatenate([c1[:, :h], s1[:, :h]], 1),
                                jnp.concatenate([-s1[:, :h], c1[:, :h]], 1)], 0)
    ci, si = c1[:h, :], s1[:h, :]
    inv_half = jnp.concatenate([jnp.concatenate([ci, -si], 1), jnp.concatenate([si, ci], 1)], 0) * (1.0 / N)
    f2 = jnp.arange(n2, dtype=I32)
    freq = f1[:, None, None] + n1 * f2[None, :, None]
    c2, s2 = cs(freq * f2[None, None, :], N)
    b_fwd = jnp.concatenate([jnp.concatenate([c2, s2], 2), jnp.concatenate([-s2, c2], 2)], 1)
    b_inv = jnp.swapaxes(b_fwd, 1, 2)
    return dict(n1=n1, n2=n2, fwd_full=fwd_full, fwd_half=fwd_half, inv_half=inv_half, b_fwd=b_fwd, b_inv=b_inv)


def _lead_kernel(*refs, has_inv, has_fwd, has_scale, precise):
    refs = list(refs)
    mm = _dot3 if precise else _bdot
    if has_inv:
        yp_ref, inv_ref, xga_ref, xgb_ref, ua_ref, ub_ref, bias_ref = refs[:7]
        refs = refs[7:]
        yp = yp_ref[...]
        ystk = yp.reshape(yp.shape[0] * yp.shape[1], yp.shape[2])
        y = mm(inv_ref[...], ystk)
        k = y.shape[0] // 2
        bias = bias_ref[...]
        za = xga_ref[...] * (y[:k] + bias * ua_ref[...])
        zb = xgb_ref[...] * (y[k:] + bias * ub_ref[...])
    else:
        xa_ref, xb_ref = refs[:2]
        refs = refs[2:]
        za, zb = xa_ref[...], xb_ref[...]
        if has_scale:
            sc = refs[0][...]
            refs = refs[1:]
            za, zb = za * sc, zb * sc
    if has_fwd:
        fwd_ref = refs[0]
        refs = refs[1:]
    if has_inv:
        refs[0][...] = za
        refs[1][...] = zb
        refs = refs[2:]
    if has_fwd:
        y_o = refs[0]
        yy = mm(fwd_ref[...], jnp.concatenate([za, zb], axis=0))
        y_o[...] = yy.reshape(y_o.shape).astype(y_o.dtype)


def _lead_call(*, pairs, n1, kin, lanes, wl, inv=None, fwd=None, xa=None, xb_off=None, scale=None,
               gate=None, precise=False, name="hy_lead"):
    nj = lanes // wl
    in_specs, args = [], []
    out_specs, out_shape = [], []
    has_inv = gate is not None
    if has_inv:
        kh = gate["xg"].shape[1]
        off = gate["off"]
        sa = pl.BlockSpec((None, kh, wl), lambda p, j: (p, 0, j))
        sb = pl.BlockSpec((None, kh, wl), lambda p, j: (p + off, 0, j))
        in_specs += [pl.BlockSpec((None, 2, n1, wl), lambda p, j: (p, 0, 0, j)),
                     pl.BlockSpec(inv.shape, lambda p, j: (0, 0)), sa, sb, sa, sb,
                     pl.BlockSpec((1, wl), lambda p, j: (0, j))]
        args += [gate["yp"], inv, gate["xg"], gate["xg"], gate["u"], gate["u"], gate["bias"]]
    else:
        in_specs += [pl.BlockSpec((None, kin, wl), lambda p, j: (p, 0, j)),
                     pl.BlockSpec((None, kin, wl), lambda p, j: (p + xb_off, 0, j))]
        args += [xa, xa]
        if scale is not None:
            in_specs.append(pl.BlockSpec((1, wl), lambda p, j: (0, j)))
            args.append(scale)
    if fwd is not None:
        in_specs.append(pl.BlockSpec(fwd.shape, lambda p, j: (0, 0)))
        args.append(fwd)
    if has_inv:
        out_specs += [sa, sa]
        out_shape += [SDS((pairs, kh, lanes), F32)] * 2
    if fwd is not None:
        out_specs.append(pl.BlockSpec((None, 2, n1, wl), lambda p, j: (p, 0, 0, j)))
        out_shape.append(SDS((pairs, 2, n1, lanes), F32 if precise else BF16))
    return pl.pallas_call(
        functools.partial(_lead_kernel, has_inv=has_inv, has_fwd=fwd is not None,
                          has_scale=scale is not None, precise=precise),
        grid=(pairs, nj), in_specs=in_specs, out_specs=out_specs, out_shape=out_shape,
        compiler_params=_cparams(("parallel", "parallel")), name=name)(*args)


def _inner_kernel(y_ref, bf_ref, *rest, f1t, n2, has_inv, precise):
    mm = _dot3 if precise else _bdot
    if has_inv:
        k_ref, bi_ref, o_ref = rest
    else:
        (o_ref,) = rest
    for i in range(f1t):
        ys = jnp.concatenate([y_ref[0, i], y_ref[1, i]], axis=0)
        x = mm(bf_ref[i], ys)
        xr, xi = x[:n2], x[n2:]
        if has_inv:
            kr, ki = k_ref[0, i], k_ref[1, i]
            zr = xr * kr - xi * ki
            zi = xr * ki + xi * kr
            yp = mm(bi_ref[i], jnp.concatenate([zr, zi], axis=0))
            o_ref[0, i] = yp[:n2].astype(o_ref.dtype)
            o_ref[1, i] = yp[n2:].astype(o_ref.dtype)
        else:
            o_ref[0, i] = xr
            o_ref[1, i] = xi


def _inner_call(y5, b_fwd, b_inv=None, kf=None, kblk=0, precise=False, f1t=8, name="hy_inner"):
    P, _, n1, n2, ch = y5.shape
    has_inv = kf is not None
    ys = pl.BlockSpec((None, 2, f1t, n2, ch), lambda j, p: (p, 0, j, 0, 0))
    ts = pl.BlockSpec((f1t, 2 * n2, 2 * n2), lambda j, p: (j, 0, 0))
    in_specs = [ys, ts]
    args = [y5, b_fwd]
    if has_inv:
        in_specs += [pl.BlockSpec((2, f1t, n2, ch), lambda j, p: (0, j, 0, kblk)), ts]
        args += [kf, b_inv]
    return pl.pallas_call(
        functools.partial(_inner_kernel, f1t=f1t, n2=n2, has_inv=has_inv, precise=precise),
        grid=(n1 // f1t, P), in_specs=in_specs, out_specs=ys, out_shape=SDS(y5.shape, y5.dtype),
        compiler_params=_cparams(("parallel", "parallel")), name=name)(*args)


def _hy_spectrum(L, hw, tabs):
    N = 2 * L
    n1, n2 = tabs["n1"], tabs["n2"]
    dh = hw["dh"]
    hwin, asum = _hy_filter(L, hw)
    ks, ss = [], []
    for o in range(2):
        kf = hwin[:, (2 * o) * dh:(2 * o + 1) * dh]
        kb = hwin[:, (2 * o + 1) * dh:(2 * o + 2) * dh]
        ks.append(jnp.concatenate([kf, jnp.zeros((1, dh), F32), kb[:0:-1]], axis=0))
        ss.append(asum[:, (2 * o) * dh:(2 * o + 1) * dh] + asum[:, (2 * o + 1) * dh:(2 * o + 2) * dh])
    k2 = jnp.concatenate(ks, axis=1)
    inv_mass = 1.0 / jnp.concatenate(ss, axis=1)
    lanes = n2 * 2 * dh
    kin = jnp.stack([k2.reshape(n1, lanes), jnp.zeros((n1, lanes), F32)])
    (y,) = _lead_call(pairs=1, n1=n1, kin=n1, lanes=lanes, wl=2048, fwd=tabs["fwd_full"], xa=kin, xb_off=1,
                      scale=jnp.tile(inv_mass, (1, n2)), precise=True, name="hy_filter_lead")
    kf = _inner_call(y.reshape(1, 2, n1, n2, 2 * dh), tabs["b_fwd"], precise=True, name="hy_filter_inner")
    return kf[0]


def _hyena_mixer(z, B, L, hw, tabs, kf):
    dh = hw["dh"]
    n1, n2 = tabs["n1"], tabs["n2"]
    kh = n1 // 2
    lanes = n2 * dh
    P = B // 2
    wl = 2048
    v, x1, x2 = _hy_pre(z, B, L, hw)
    view = lambda a: a.reshape(B, kh, lanes)
    v3, x13, x23 = view(v), view(x1), view(x2)
    bias = [jnp.tile(hw["bias"][o:o + 1], (1, n2)) for o in range(2)]
    (y,) = _lead_call(pairs=P, n1=n1, kin=kh, lanes=lanes, wl=wl, fwd=tabs["fwd_half"], xa=v3, xb_off=P,
                      name="hy_lead0")
    yp = _inner_call(y.reshape(P, 2, n1, n2, dh), tabs["b_fwd"], tabs["b_inv"], kf, kblk=0, name="hy_inner0")
    z1a, z1b, y = _lead_call(pairs=P, n1=n1, kin=kh, lanes=lanes, wl=wl, inv=tabs["inv_half"], fwd=tabs["fwd_half"],
                             gate=dict(yp=yp.reshape(P, 2, n1, lanes), xg=x13, u=v3, bias=bias[0], off=P),
                             name="hy_lead1")
    z1 = jnp.concatenate([z1a, z1b], axis=0)
    yp = _inner_call(y.reshape(P, 2, n1, n2, dh), tabs["b_fwd"], tabs["b_inv"], kf, kblk=1, name="hy_inner1")
    ya, yb = _lead_call(pairs=P, n1=n1, kin=kh, lanes=lanes, wl=wl, inv=tabs["inv_half"],
                        gate=dict(yp=yp.reshape(P, 2, n1, lanes), xg=x23, u=z1, bias=bias[1], off=P),
                        name="hy_lead2")
    return jnp.concatenate([ya, yb], axis=0).reshape(B * L, dh)


def _outproj_kernel(yr_ref, ys_ref, yh_ref, x_ref, wo_ref, gh_ref, gf_ref, wr_ref, xo_ref, xn_ref, aff_ref,
                    *, dr, ds):
    yh = _rms(yh_ref[...], gh_ref[...])
    acc = _bdot(yr_ref[...], wo_ref[0:dr, :])
    acc += _bdot(ys_ref[...], wo_ref[dr:dr + ds, :])
    acc += _bdot(yh, wo_ref[dr + ds:, :])
    xnew = x_ref[...] + acc
    xo_ref[...] = xnew
    xn = _rms(xnew, gf_ref[...])
    xn_ref[...] = xn
    logits = _dot3_t(wr_ref[...], xn)
    m = jnp.max(logits, axis=0, keepdims=True)
    e = jnp.exp(logits - m)
    aff_ref[...] = e / jnp.sum(e, axis=0, keepdims=True)


def _out_proj(yr, ys, yh, x, lw, tm=512):
    T, D = x.shape
    dr, ds, dh = yr.shape[1], ys.shape[1], yh.shape[1]
    E = lw["w_router_t"].shape[0]
    row = lambda i: (i, 0)
    const = lambda i: (0, 0)
    return pl.pallas_call(
        functools.partial(_outproj_kernel, dr=dr, ds=ds),
        grid=(T // tm,),
        in_specs=[pl.BlockSpec((tm, dr), row), pl.BlockSpec((tm, ds), row), pl.BlockSpec((tm, dh), row),
                  pl.BlockSpec((tm, D), row), pl.BlockSpec((D, D), const), pl.BlockSpec((1, dh), const),
                  pl.BlockSpec((1, D), const), pl.BlockSpec((E, D), const)],
        out_specs=[pl.BlockSpec((tm, D), row), pl.BlockSpec((tm, D), row), pl.BlockSpec((E, tm), lambda i: (0, i))],
        out_shape=[SDS((T, D), F32), SDS((T, D), F32), SDS((E, T), F32)],
        compiler_params=_cparams(("parallel",)), name="out_proj")(
            yr, ys, yh, x, lw["w_out"], lw["hy"]["norm_g"], lw["norm_ffn_g"], lw["w_router_t"])


def _select_prefix_kernel(aff_ref, incl_ref, off_ref, *, cap):
    aff = aff_ref[...]
    nt = aff.shape[0]
    bits = pltpu.bitcast(aff, I32)

    def body(i, thr):
        cand = thr | jnp.left_shift(jnp.int32(1), 30 - i)
        cnt = jnp.sum(jnp.where(bits >= cand, 1.0, 0.0))
        return jnp.where(cnt >= cap, cand, thr)

    thr = lax.fori_loop(0, 31, body, jnp.int32(0))
    gt = bits > thr
    eq = bits == thr
    need = cap - jnp.sum(jnp.where(gt, 1.0, 0.0))

    li = lax.broadcasted_iota(I32, (V7X_LANES, V7X_LANES), 0)
    mi = lax.broadcasted_iota(I32, (V7X_LANES, V7X_LANES), 1)
    upper = jnp.where(li <= mi, 1.0, 0.0).astype(BF16)
    ones = jnp.ones((V7X_LANES, V7X_LANES), BF16)
    ri = lax.broadcasted_iota(I32, (nt, nt), 0)
    ci = lax.broadcasted_iota(I32, (nt, nt), 1)
    lower = jnp.where(ci < ri, 1.0, 0.0).astype(BF16)
    dot = functools.partial(jnp.dot, preferred_element_type=F32)

    def prefix(m):
        mb = m.astype(BF16)
        incl = dot(mb, upper)
        tot = dot(mb, ones)
        return incl, dot(lower, tot.astype(BF16))

    eqf = jnp.where(eq, 1.0, 0.0)
    incl_eq, off_eq = prefix(eqf)
    rank_eq = off_eq + incl_eq - eqf
    sel = jnp.where(gt, 1.0, jnp.where(eq & (rank_eq < need), 1.0, 0.0))
    incl, off = prefix(sel)
    incl_ref[...] = incl
    off_ref[...] = off


def _select_index_kernel(incl_ref, off_ref, aff_ref, idx_ref, gate_ref, *, sb):
    j0 = pl.program_id(1) * sb
    incl = incl_ref[...]
    off = off_ref[...]
    nt = incl.shape[0]
    off_col = off[:, 0:1]
    end_col = off_col + incl[:, V7X_LANES - 1:V7X_LANES]
    j = (j0 + lax.broadcasted_iota(I32, (1, sb), 1)).astype(F32)
    tile_id = jnp.sum(jnp.where(end_col <= j, 1.0, 0.0), axis=0, keepdims=True)
    ti = lax.broadcasted_iota(I32, (nt, sb), 0).astype(F32)
    onehot = jnp.where(ti == tile_id, 1.0, 0.0)
    excl = jnp.sum(onehot * off_col, axis=0, keepdims=True)
    rj = j - excl
    ohb = onehot.astype(BF16)
    dot = functools.partial(jnp.dot, preferred_element_type=F32)
    rows = dot(incl.T.astype(BF16), ohb)
    local = jnp.sum(jnp.where(rows <= rj, 1.0, 0.0), axis=0, keepdims=True)
    idx_ref[...] = (tile_id * V7X_LANES + local).astype(I32)
    ah, am, al = _split3(aff_ref[...].T)
    arows = dot(ah, ohb) + dot(am, ohb) + dot(al, ohb)
    lane = lax.broadcasted_iota(I32, (V7X_LANES, sb), 0).astype(F32)
    gate_ref[...] = jnp.sum(jnp.where(lane == local, arows, 0.0), axis=0, keepdims=True)


def _moe_select(aff_t, cap, sb=1024):
    E, T = aff_t.shape
    nt = T // V7X_LANES
    aff3 = aff_t.reshape(E, nt, V7X_LANES)
    s3 = pl.BlockSpec((None, nt, V7X_LANES), lambda e: (e, 0, 0))
    incl, off = pl.pallas_call(
        functools.partial(_select_prefix_kernel, cap=cap),
        grid=(E,), in_specs=[s3], out_specs=[s3, s3],
        out_shape=[SDS((E, nt, V7X_LANES), F32)] * 2,
        compiler_params=_cparams(("parallel",)), name="moe_select_prefix")(aff3)
    s3b = pl.BlockSpec((None, nt, V7X_LANES), lambda e, j: (e, 0, 0))
    so = pl.BlockSpec((None, 1, sb), lambda e, j: (e, 0, j))
    idx, gate = pl.pallas_call(
        functools.partial(_select_index_kernel, sb=sb),
        grid=(E, cap // sb), in_specs=[s3b, s3b, s3b], out_specs=[so, so],
        out_shape=[SDS((E, 1, cap), I32), SDS((E, 1, cap), F32)],
        compiler_params=_cparams(("parallel", "parallel")), name="moe_select_index")(incl, off, aff3)
    return idx, gate


def _moe_ffn_kernel(idx_hbm, gate_ref, xn_hbm, acc_in_hbm, wg_ref, wu_ref, wd_ref, acc_hbm,
                    idx_s, xbuf, abuf, sem_i, sem_x, sem_a, sem_s, *, m, nt):
    del acc_in_hbm
    hm = m // 2
    cp = pltpu.make_async_copy(idx_hbm.at[pl.program_id(0) * nt + pl.program_id(1)], idx_s, sem_i)
    cp.start()
    cp.wait()

    def rows(k):
        return pl.ds(k * hm, hm)

    for k in range(2):
        def issue(r, carry, k=k):
            t = idx_s[k * hm + r]
            pltpu.make_async_copy(xn_hbm.at[t], xbuf.at[k * hm + r], sem_x.at[k]).start()
            pltpu.make_async_copy(acc_hbm.at[t], abuf.at[k * hm + r], sem_a.at[k]).start()
            return carry

        lax.fori_loop(0, hm, issue, 0, unroll=8)

    g_col = jnp.broadcast_to(gate_ref[...], (V7X_LANES, m)).T[:, 0:1]
    for k in range(2):
        pltpu.make_async_copy(xn_hbm.at[pl.ds(0, hm)], xbuf.at[rows(k)], sem_x.at[k]).wait()
        x = xbuf[rows(k), :].astype(BF16)
        hg = jnp.dot(x, wg_ref[...], preferred_element_type=F32)
        hu = jnp.dot(x, wu_ref[...], preferred_element_type=F32)
        h = hg * _sigmoid(hg) * hu
        y = jnp.dot(h.astype(BF16), wd_ref[...], preferred_element_type=F32)
        pltpu.make_async_copy(acc_hbm.at[pl.ds(0, hm)], abuf.at[rows(k)], sem_a.at[k]).wait()
        abuf[rows(k), :] = abuf[rows(k), :] + y * g_col[k * hm:(k + 1) * hm]

        def scatter(r, carry, k=k):
            pltpu.make_async_copy(abuf.at[k * hm + r], acc_hbm.at[idx_s[k * hm + r]], sem_s.at[k]).start()
            return carry

        lax.fori_loop(0, hm, scatter, 0, unroll=8)

    for k in range(2):
        pltpu.make_async_copy(abuf.at[rows(k)], acc_hbm.at[pl.ds(0, hm)], sem_s.at[k]).wait()


def _moe_ffn(idx, gate, xn, acc, mw, m=1024):
    T, D = xn.shape
    E, _, cap = idx.shape
    nt = cap // m
    F = mw["w_gate"].shape[2]
    idx2 = idx.reshape(E * nt, m)
    gate3 = gate.reshape(E * nt, 1, m)
    wspec = lambda a, b: pl.BlockSpec((None, a, b), lambda e, i: (e, 0, 0))
    any_spec = pl.BlockSpec(memory_space=pl.ANY)
    return pl.pallas_call(
        functools.partial(_moe_ffn_kernel, m=m, nt=nt),
        grid=(E, nt),
        in_specs=[any_spec, pl.BlockSpec((None, 1, m), lambda e, i: (e * nt + i, 0, 0)), any_spec, any_spec,
                  wspec(D, F), wspec(D, F), wspec(F, D)],
        out_specs=any_spec,
        out_shape=SDS((T, D), F32),
        input_output_aliases={3: 0},
        scratch_shapes=[pltpu.SMEM((m,), I32), pltpu.VMEM((m, D), F32), pltpu.VMEM((m, D), F32),
                        pltpu.SemaphoreType.DMA, pltpu.SemaphoreType.DMA((2,)), pltpu.SemaphoreType.DMA((2,)),
                        pltpu.SemaphoreType.DMA((2,))],
        compiler_params=_cparams(("arbitrary", "arbitrary")), name="moe_ffn")(
            idx2, gate3, xn, acc, mw["w_gate"], mw["w_up"], mw["w_down"])


def _moe(xnew, xn, aff_t, mw):
    T = xn.shape[0]
    E = aff_t.shape[0]
    cap = EC_CAPACITY_FACTOR * T // E
    idx, gate = _moe_select(aff_t, cap)
    return _moe_ffn(idx, gate, xn, xnew, mw)


def _norm_kernel(x_ref, g_ref, o_ref):
    o_ref[...] = _rms(x_ref[...], g_ref[...])


def _final_norm(x, g, tm=512):
    T, D = x.shape
    return pl.pallas_call(
        _norm_kernel, grid=(T // tm,),
        in_specs=[pl.BlockSpec((tm, D), lambda i: (i, 0)), pl.BlockSpec((1, D), lambda i: (0, 0))],
        out_specs=pl.BlockSpec((tm, D), lambda i: (i, 0)), out_shape=SDS((T, D), F32),
        compiler_params=_cparams(("parallel",)), name="final_norm")(x, g)


def _block_diag_rows(blocks):
    n, r, c = blocks.shape
    eye = jnp.eye(n, dtype=blocks.dtype)
    return jnp.einsum("nrc,nm->nrmc", blocks, eye).reshape(n * r, n * c)


def _rwkv_weights(i, p):
    heads, hd = p["rwkv_r_k"].shape[1:]
    dr = heads * hd
    row = lambda a: a.reshape(1, -1)
    head_id = jnp.arange(dr) // hd
    return dict(
        dr=dr, heads=heads, hd=hd,
        mu_prev=row(p["rwkv_mu_prev"][i]), mu_next=row(p["rwkv_mu_next"][i]),
        w0=row(p["rwkv_w0"][i]), wup=_block_diag_rows(p["rwkv_w_up"][i]).astype(BF16),
        a0=row(p["rwkv_a0"][i]), aup=_block_diag_rows(p["rwkv_a_up"][i]).astype(BF16),
        gup=p["rwkv_g_up"][i].astype(BF16),
        k_k=row(p["rwkv_k_k"][i]), k_a=row(p["rwkv_k_a"][i]), r_k=row(p["rwkv_r_k"][i]),
        ln_w=row(p["rwkv_ln_w"][i]), ln_b=row(p["rwkv_ln_b"][i]),
        hsum=(head_id[:, None] == head_id[None, :]).astype(BF16))


def _s5_weights(i, p):
    lam_re, lam_im, log_dt = p["s5_lam_re"][i], p["s5_lam_im"][i], p["s5_log_dt"][i]
    b_re, b_im, c_re, c_im = p["s5_b_re"][i], p["s5_b_im"][i], p["s5_c_re"][i], p["s5_c_im"][i]
    G, P, H = b_re.shape
    l_re = jnp.minimum(lam_re, -1e-4)
    dt = jnp.exp(log_dt)[..., None]
    mag = jnp.exp(l_re * dt)
    ab_re = mag * jnp.cos(lam_im * dt)
    ab_im = mag * jnp.sin(lam_im * dt)
    den = l_re * l_re + lam_im * lam_im
    n_re = ab_re - 1.0
    f_re = (n_re * l_re + ab_im * lam_im) / den
    f_im = (ab_im * l_re - n_re * lam_im) / den
    w_re = jnp.swapaxes(f_re[..., None] * b_re - f_im[..., None] * b_im, 2, 3)
    w_im = jnp.swapaxes(f_re[..., None] * b_im + f_im[..., None] * b_re, 2, 3)
    win = jnp.stack([jnp.concatenate([_block_diag_rows(w_re[d]), _block_diag_rows(w_im[d])], axis=1)
                     for d in range(2)]).astype(BF16)
    cout = jnp.concatenate([_block_diag_rows(jnp.swapaxes(c_re, 1, 2)),
                            _block_diag_rows(jnp.swapaxes(-c_im, 1, 2))], axis=0).astype(BF16)
    ns = G * P
    return dict(ns=ns, cu=G * H, win=win, cout=cout,
                a_re=ab_re.reshape(2, 1, ns), a_im=ab_im.reshape(2, 1, ns),
                d=p["s5_d"][i].reshape(1, -1), w_glu=p["s5_w_glu"][i].astype(BF16),
                norm_g=p["s5_norm_g"][i].reshape(1, -1))


def _hyena_weights(i, p):
    dh = p["hy_norm_g"].shape[1]
    emb = p["hy_f_w1"].shape[1]
    w1 = jnp.pad(p["hy_f_w1"][i], ((0, V7X_LANES - emb), (0, 0)))
    deltas = jnp.linspace(math.log(HYENA_TARGET) / HYENA_FAST_DECAY, math.log(HYENA_TARGET) / HYENA_SLOW_DECAY,
                          dh, dtype=F32)
    ncol = p["hy_f_w3"].shape[2]
    col_dir = (jnp.arange(ncol) // dh) % 2
    row = lambda a: a.reshape(1, -1)
    return dict(dh=dh, conv_w=p["hy_conv_w"][i], conv_b=row(p["hy_conv_b"][i]),
                f_w1=w1, f_b1=row(p["hy_f_b1"][i]), f_w2=p["hy_f_w2"][i], f_b2=row(p["hy_f_b2"][i]),
                f_w3=p["hy_f_w3"][i], f_b3=row(p["hy_f_b3"][i]), f_freq=row(p["hy_f_freq"][i]),
                abs_deltas=jnp.tile(jnp.abs(deltas), ncol // dh).reshape(1, -1),
                bwd_mask=col_dir.astype(F32).reshape(1, -1),
                bias=p["hy_bias"][i], norm_g=row(p["hy_norm_g"][i]))


def _layer_weights(i, p):
    return dict(
        norm_mix_g=p["norm_mix_g"][i].reshape(1, -1), w_in=p["w_in"][i].astype(BF16),
        rwkv=_rwkv_weights(i, p), s5=_s5_weights(i, p), hy=_hyena_weights(i, p),
        w_out=p["w_out"][i].astype(BF16), norm_ffn_g=p["norm_ffn_g"][i].reshape(1, -1),
        w_router_t=p["moe_w_router"][i].T,
        moe=dict(w_gate=p["moe_w_gate"][i].astype(BF16), w_up=p["moe_w_up"][i].astype(BF16),
                 w_down=p["moe_w_down"][i].astype(BF16)))


def _layer(x, B, L, lw, tabs, kf):
    c_r = lw["rwkv"]["mu_prev"].shape[1]
    c_s = lw["s5"]["cu"]
    c_h = 3 * lw["hy"]["dh"]
    z_r, z_s, z_h = _in_proj(x, B, L, lw["norm_mix_g"], lw["w_in"], c_r, c_s, c_h)
    y_r = _rwkv_mixer(z_r, B, L, lw["rwkv"])
    y_s = _s5_mixer(z_s, B, L, lw["s5"])
    y_h = _hyena_mixer(z_h, B, L, lw["hy"], tabs, kf)
    xnew, xn, aff_t = _out_proj(y_r, y_s, y_h, x, lw)
    return _moe(xnew, xn, aff_t, lw["moe"])


def _trunk(x, layers, final_g):
    B, L, D = x.shape
    tabs = _dft_tables(2 * L)
    h = x.reshape(B * L, D)
    for lw in layers:
        kf = _hy_spectrum(L, lw["hy"], tabs)
        h = _layer(h, B, L, lw, tabs, kf)
    return _final_norm(h, final_g.reshape(1, -1)).reshape(B, L, D)


def kernel(x_prompt, x_sample, norm_mix_g, w_in, rwkv_mu_prev, rwkv_mu_next, rwkv_w0, rwkv_w_up, rwkv_a0, rwkv_a_up, rwkv_g_up, rwkv_k_k, rwkv_k_a, rwkv_r_k, rwkv_ln_w, rwkv_ln_b, s5_lam_re, s5_lam_im, s5_log_dt, s5_b_re, s5_b_im, s5_c_re, s5_c_im, s5_d, s5_w_glu, s5_norm_g, hy_conv_w, hy_conv_b, hy_f_w1, hy_f_b1, hy_f_w2, hy_f_b2, hy_f_w3, hy_f_b3, hy_f_freq, hy_bias, hy_norm_g, w_out, norm_ffn_g, moe_w_router, moe_w_gate, moe_w_up, moe_w_down, final_norm_g):
    p = dict(norm_mix_g=norm_mix_g, w_in=w_in, rwkv_mu_prev=rwkv_mu_prev, rwkv_mu_next=rwkv_mu_next,
             rwkv_w0=rwkv_w0, rwkv_w_up=rwkv_w_up, rwkv_a0=rwkv_a0, rwkv_a_up=rwkv_a_up,
             rwkv_g_up=rwkv_g_up, rwkv_k_k=rwkv_k_k, rwkv_k_a=rwkv_k_a, rwkv_r_k=rwkv_r_k,
             rwkv_ln_w=rwkv_ln_w, rwkv_ln_b=rwkv_ln_b, s5_lam_re=s5_lam_re, s5_lam_im=s5_lam_im,
             s5_log_dt=s5_log_dt, s5_b_re=s5_b_re, s5_b_im=s5_b_im, s5_c_re=s5_c_re, s5_c_im=s5_c_im,
             s5_d=s5_d, s5_w_glu=s5_w_glu, s5_norm_g=s5_norm_g, hy_conv_w=hy_conv_w, hy_conv_b=hy_conv_b,
             hy_f_w1=hy_f_w1, hy_f_b1=hy_f_b1, hy_f_w2=hy_f_w2, hy_f_b2=hy_f_b2, hy_f_w3=hy_f_w3,
             hy_f_b3=hy_f_b3, hy_f_freq=hy_f_freq, hy_bias=hy_bias, hy_norm_g=hy_norm_g, w_out=w_out,
             norm_ffn_g=norm_ffn_g, moe_w_router=moe_w_router, moe_w_gate=moe_w_gate,
             moe_w_up=moe_w_up, moe_w_down=moe_w_down)
    layers = [_layer_weights(i, p) for i in range(w_in.shape[0])]
    return (_trunk(x_prompt, layers, final_norm_g), _trunk(x_sample, layers, final_norm_g))
```

```python
import functools
import math

import jax
import jax.numpy as jnp
from jax import lax
from jax.experimental import pallas as pl
from jax.experimental.pallas import tpu as pltpu

F32 = jnp.float32
BF16 = jnp.bfloat16
I32 = jnp.int32
SDS = jax.ShapeDtypeStruct

EPS_NORM = 1e-6
EPS_GN = 64e-5
HYENA_FAST_DECAY = 0.3
HYENA_SLOW_DECAY = 1.5
HYENA_TARGET = 1e-2
EC_CAPACITY_FACTOR = 2

V7X_LANES = 128
V7X_SUBLANES = 8
V7X_VMEM_LIMIT_BYTES = 56 * 1024 * 1024

RWKV_CHUNK = 64
DFT_INNER = 64


def _cparams(sem, vmem=V7X_VMEM_LIMIT_BYTES):
    return pltpu.CompilerParams(dimension_semantics=sem, vmem_limit_bytes=vmem)


def _bdot(a, b):
    return jnp.dot(a.astype(BF16), b.astype(BF16), preferred_element_type=F32)


def _bdot_t(a, b):
    return lax.dot_general(a.astype(BF16), b.astype(BF16), (((1,), (1,)), ((), ())),
                           preferred_element_type=F32)


def _split2(x):
    hi = x.astype(BF16)
    lo = (x - hi.astype(F32)).astype(BF16)
    return hi, lo


def _split3(x):
    hi = x.astype(BF16)
    r1 = x - hi.astype(F32)
    mid = r1.astype(BF16)
    lo = (r1 - mid.astype(F32)).astype(BF16)
    return hi, mid, lo


def _dot3(a, b):
    ah, al = _split2(a)
    bh, bl = _split2(b)
    d = functools.partial(jnp.dot, preferred_element_type=F32)
    return d(ah, bh) + d(ah, bl) + d(al, bh)


def _dot3_t(a, b):
    ah, al = _split2(a)
    bh, bl = _split2(b)
    d = lambda x, y: lax.dot_general(x, y, (((1,), (1,)), ((), ())), preferred_element_type=F32)
    return d(ah, bh) + d(ah, bl) + d(al, bh)


def _dot_exact_rhs(m01, x):
    xh, xl = _split2(x)
    d = functools.partial(jnp.dot, preferred_element_type=F32)
    return d(m01, xh) + d(m01, xl)


def _dot_exact_lhs(x, m01):
    xh, xl = _split2(x)
    d = functools.partial(jnp.dot, preferred_element_type=F32)
    return d(xh, m01) + d(xl, m01)


def _rms(x, g):
    return x * lax.rsqrt(jnp.mean(x * x, axis=-1, keepdims=True) + EPS_NORM) * g


def _sigmoid(x):
    return 1.0 / (1.0 + jnp.exp(-x))


def _softplus(x):
    return jnp.maximum(x, 0.0) + jnp.log(1.0 + jnp.exp(-jnp.abs(x)))


def _shift_rows(z, prev_row, next_row):
    n = z.shape[0]
    rows = lax.broadcasted_iota(I32, z.shape, 0)
    zprev = jnp.where(rows == 0, prev_row, pltpu.roll(z, 1, 0))
    znext = jnp.where(rows == n - 1, next_row, pltpu.roll(z, n - 1, 0))
    return zprev, znext


def _halo_specs(tm, cols, nl, total_rows):
    r8 = tm // V7X_SUBLANES
    last = total_rows // V7X_SUBLANES - 1
    prev = pl.BlockSpec((V7X_SUBLANES, cols), lambda b, l: (jnp.maximum((b * nl + l) * r8 - 1, 0), 0))
    nxt = pl.BlockSpec((V7X_SUBLANES, cols), lambda b, l: (jnp.minimum((b * nl + l + 1) * r8, last), 0))
    return prev, nxt


def _halo_rows(zp_ref, zn_ref):
    l = pl.program_id(1)
    nl = pl.num_programs(1)
    prev_row = jnp.where(l == 0, 0.0, zp_ref[V7X_SUBLANES - 1:V7X_SUBLANES, :])
    next_row = jnp.where(l == nl - 1, 0.0, zn_ref[0:1, :])
    return prev_row, next_row


def _inproj_kernel(x_ref, g_ref, w_ref, zr_ref, zs_ref, zh_ref, *, c_r, c_s):
    h = _rms(x_ref[...], g_ref[...])
    z = _bdot(h, w_ref[...])
    zr_ref[...] = z[:, :c_r]
    zs_ref[...] = z[:, c_r:c_r + c_s]
    zh_ref[...] = z[:, c_r + c_s:]


def _in_proj(x, B, L, g, w, c_r, c_s, c_h, tm=512):
    T, D = x.shape
    nl = L // tm
    return pl.pallas_call(
        functools.partial(_inproj_kernel, c_r=c_r, c_s=c_s),
        grid=(B, nl),
        in_specs=[pl.BlockSpec((tm, D), lambda b, l: (b * nl + l, 0)),
                  pl.BlockSpec((1, D), lambda b, l: (0, 0)),
                  pl.BlockSpec((D, c_r + c_s + c_h), lambda b, l: (0, 0))],
        out_specs=[pl.BlockSpec((tm, c_r), lambda b, l: (b * nl + l, 0)),
                   pl.BlockSpec((tm, c_s), lambda b, l: (l, b)),
                   pl.BlockSpec((tm, c_h), lambda b, l: (b * nl + l, 0))],
        out_shape=[SDS((T, c_r), F32), SDS((L, B * c_s), F32), SDS((T, c_h), F32)],
        compiler_params=_cparams(("parallel", "parallel")), name="in_proj")(x, g, w)


def _rwkv_pre_kernel(z_ref, zp_ref, zn_ref, mup_ref, mun_ref, w0_ref, wup_ref, a0_ref, aup_ref, gup_ref,
                     kk_ref, ka_ref, hs_ref,
                     r_o, v_o, kkn_o, g_o, ks_o, lw_o, kd_o, bd_o, *, dr):
    z = z_ref[...]
    prev_row, next_row = _halo_rows(zp_ref, zn_ref)
    zprev, znext = _shift_rows(z, prev_row, next_row)
    zs = z + mup_ref[...] * (zprev - z) + mun_ref[...] * (znext - z)
    r = zs[:, 0:dr]
    k = zs[:, dr:2 * dr]
    v = zs[:, 2 * dr:3 * dr]
    c3 = 3 * dr
    wd = zs[:, c3:c3 + 128]
    ad = zs[:, c3 + 128:c3 + 256]
    gd = zs[:, c3 + 256:c3 + 384]
    w_raw = w0_ref[...] + _bdot(jnp.tanh(wd), wup_ref[...])
    lw = -jnp.exp(-_softplus(-w_raw) - 0.5)
    a = _sigmoid(a0_ref[...] + _bdot(ad, aup_ref[...]))
    g = _bdot(_sigmoid(gd), gup_ref[...])
    kk = k * kk_ref[...]
    ss = _dot_exact_lhs(kk * kk, hs_ref[...])
    kk = kk / jnp.maximum(jnp.sqrt(ss), 1e-12)
    ka = ka_ref[...]
    r_o[...] = r
    v_o[...] = v
    kkn_o[...] = kk
    g_o[...] = g
    ksum = None
    for d in range(2):
        a_d = a[:, d * dr:(d + 1) * dr]
        k_d = k * (1.0 + (a_d - 1.0) * ka)
        lw_o[d] = lw[:, d * dr:(d + 1) * dr]
        kd_o[d] = k_d
        bd_o[d] = kk * a_d
        ksum = k_d if ksum is None else ksum + k_d
    ks_o[...] = ksum


def _rwkv_pre(z, B, L, pw, tm=256):
    T, cr = z.shape
    dr = pw["dr"]
    nl = L // tm
    row = lambda b, l: (b * nl + l, 0)
    const = lambda b, l: (0, 0)
    prev, nxt = _halo_specs(tm, cr, nl, T)
    full = lambda a: pl.BlockSpec(a.shape, const)
    params = [pw["mu_prev"], pw["mu_next"], pw["w0"], pw["wup"], pw["a0"], pw["aup"], pw["gup"],
              pw["k_k"], pw["k_a"], pw["hsum"]]
    o1 = pl.BlockSpec((tm, dr), row)
    o2 = pl.BlockSpec((2, tm, dr), lambda b, l: (0, b * nl + l, 0))
    return pl.pallas_call(
        functools.partial(_rwkv_pre_kernel, dr=dr),
        grid=(B, nl),
        in_specs=[pl.BlockSpec((tm, cr), row), prev, nxt] + [full(a) for a in params],
        out_specs=[o1] * 5 + [o2] * 3,
        out_shape=[SDS((T, dr), F32)] * 5 + [SDS((2, T, dr), F32)] * 3,
        compiler_params=_cparams(("parallel", "parallel")), name="rwkv_pre")(z, z, z, *params)


def _bmm(a, b):
    return lax.dot_general(a.astype(BF16), b.astype(BF16), (((2,), (1,)), ((0,), (0,))),
                           preferred_element_type=F32)


def _bmm_t(a, b):
    return lax.dot_general(a.astype(BF16), b.astype(BF16), (((2,), (2,)), ((0,), (0,))),
                           preferred_element_type=F32)


def _rwkv_chunk_operands(d, r_ref, v_ref, kk_ref, lw_ref, kd_ref, bd_ref, *, heads, hd):
    C = r_ref.shape[0]
    ti = lax.broadcasted_iota(I32, (C, C), 0)
    si = lax.broadcasted_iota(I32, (C, C), 1)
    dd = (si - ti) if d == 0 else (ti - si)
    lw = lw_ref[...]
    cum = _dot_exact_rhs(jnp.where(dd <= 0, 1.0, 0.0).astype(BF16), lw)
    tot = jnp.sum(lw, axis=0, keepdims=True)
    kd = kd_ref[...]
    bd = bd_ref[...]
    p_inv = jnp.exp(-cum)
    p_end = jnp.exp(tot - cum)

    def split(x):
        return jnp.stack([x[:, h * hd:(h + 1) * hd] for h in range(heads)])

    return dict(
        incl=(dd <= 0)[None], strict=(dd < 0)[None], eye=jnp.where(dd == 0, 1.0, 0.0)[None],
        xq=split(jnp.concatenate([kk_ref[...] * jnp.exp(cum - lw), r_ref[...] * jnp.exp(cum)], axis=0).astype(BF16)),
        kt=split((kd * p_inv).astype(BF16)), bt=split((bd * p_inv).astype(BF16)),
        kb=split(jnp.concatenate([kd * p_end, bd * p_end], axis=0).astype(BF16)),
        v=split(v_ref[...].astype(BF16)), p_tot=split(jnp.exp(tot)))


def _rwkv_chunk_kernel(r0, v0, kk0, lw0, kd0, bd0, r1, v1, kk1, lw1, kd1, bd1, y0, y1, s_ref, *, heads, hd):
    @pl.when(pl.program_id(1) == 0)
    def _():
        s_ref[...] = jnp.zeros_like(s_ref)

    C = r0.shape[0]
    ops = [_rwkv_chunk_operands(0, r0, v0, kk0, lw0, kd0, bd0, heads=heads, hd=hd),
           _rwkv_chunk_operands(1, r1, v1, kk1, lw1, kd1, bd1, heads=heads, hd=hd)]
    dirs = (0, 1)
    a_k = [_bmm_t(ops[d]["xq"], ops[d]["kt"]) for d in dirs]
    a_b = [_bmm_t(ops[d]["xq"], ops[d]["bt"]) for d in dirs]
    a_kq = [jnp.concatenate([jnp.where(ops[d]["strict"], a_k[d][:, :C], 0.0),
                             jnp.where(ops[d]["incl"], a_k[d][:, C:], 0.0)], axis=1) for d in dirs]
    a_rb = [jnp.where(ops[d]["incl"], a_b[d][:, C:], 0.0) for d in dirs]
    n1 = [jnp.where(ops[d]["strict"], a_b[d][:, :C], 0.0) for d in dirs]
    tinv = [ops[d]["eye"] - n1[d] for d in dirs]
    npow = [_bmm(n1[d], n1[d]) for d in dirs]
    for _ in range(int(math.log2(C)) - 2):
        both = [_bmm(jnp.concatenate([tinv[d], npow[d]], axis=1), npow[d]) for d in dirs]
        tinv = [tinv[d] + both[d][:, :C] for d in dirs]
        npow = [both[d][:, C:] for d in dirs]
    tinv = [tinv[d] + _bmm(tinv[d], npow[d]) for d in dirs]
    s_old = [s_ref[d] for d in dirs]
    xs = [_bmm_t(ops[d]["xq"], s_old[d]) for d in dirs]
    av = [_bmm(a_kq[d], ops[d]["v"]) for d in dirs]
    u = [_bmm(tinv[d], -(xs[d][:, :C] + av[d][:, :C])) for d in dirs]
    o = [xs[d][:, C:] + av[d][:, C:] + _bmm(a_rb[d], u[d]) for d in dirs]
    for d, y_ref in zip(dirs, (y0, y1)):
        vu = jnp.concatenate([ops[d]["v"].astype(F32), u[d]], axis=1)
        upd = lax.dot_general(vu.astype(BF16), ops[d]["kb"], (((1,), (1,)), ((0,), (0,))),
                              preferred_element_type=F32)
        s_ref[d] = s_old[d] * ops[d]["p_tot"] + upd
        for h in range(heads):
            y_ref[:, h * hd:(h + 1) * hd] = o[d][h]


def _rwkv_chunk(r, v, kk, lw, kd, bd, B, L, heads, hd):
    T, dr = r.shape
    C = RWKV_CHUNK
    nc = L // C
    f1 = pl.BlockSpec((C, dr), lambda b, c: (b * nc + c, 0))
    b1 = pl.BlockSpec((C, dr), lambda b, c: (b * nc + nc - 1 - c, 0))
    f2 = pl.BlockSpec((None, C, dr), lambda b, c: (0, b * nc + c, 0))
    b2 = pl.BlockSpec((None, C, dr), lambda b, c: (1, b * nc + nc - 1 - c, 0))
    return pl.pallas_call(
        functools.partial(_rwkv_chunk_kernel, heads=heads, hd=hd),
        grid=(B, nc),
        in_specs=[f1, f1, f1, f2, f2, f2, b1, b1, b1, b2, b2, b2],
        out_specs=[f1, b1],
        out_shape=[SDS((T, dr), F32)] * 2,
        scratch_shapes=[pltpu.VMEM((2, heads, hd, hd), F32)],
        compiler_params=_cparams(("parallel", "arbitrary")), name="rwkv_chunk")(
            r, v, kk, lw, kd, bd, r, v, kk, lw, kd, bd)


def _rwkv_post_kernel(y0_ref, y1_ref, r_ref, v_ref, ks_ref, g_ref, rk_ref, lnw_ref, lnb_ref, hs_ref, o_ref, *, hd):
    y = y0_ref[...] + y1_ref[...]
    hs = hs_ref[...]
    mean = _dot_exact_lhs(y, hs) * (1.0 / hd)
    yc = y - mean
    var = _dot_exact_lhs(yc * yc, hs) * (1.0 / hd)
    yn = yc * lax.rsqrt(var + EPS_GN) * lnw_ref[...] + lnb_ref[...]
    bonus = _dot_exact_lhs(r_ref[...] * ks_ref[...] * rk_ref[...], hs) * v_ref[...]
    o_ref[...] = (yn + bonus) * g_ref[...]


def _rwkv_post(y0, y1, r, v, ks, g, pw, tm=512):
    T, dr = r.shape
    row = lambda i: (i, 0)
    const = lambda i: (0, 0)
    s1 = pl.BlockSpec((tm, dr), row)
    vec = pl.BlockSpec((1, dr), const)
    return pl.pallas_call(
        functools.partial(_rwkv_post_kernel, hd=pw["hd"]),
        grid=(T // tm,),
        in_specs=[s1, s1, s1, s1, s1, s1, vec, vec, vec, pl.BlockSpec((dr, dr), const)],
        out_specs=s1,
        out_shape=SDS((T, dr), F32),
        compiler_params=_cparams(("parallel",)), name="rwkv_post")(
            y0, y1, r, v, ks, g, pw["r_k"], pw["ln_w"], pw["ln_b"], pw["hsum"])


def _rwkv_mixer(z, B, L, pw):
    r, v, kk, g, ks, lw, kd, bd = _rwkv_pre(z, B, L, pw)
    y0, y1 = _rwkv_chunk(r, v, kk, lw, kd, bd, B, L, pw["heads"], pw["hd"])
    return _rwkv_post(y0, y1, r, v, ks, g, pw)


def _s5_scan_kernel(u_ref, win_ref, are_ref, aim_ref, cout_ref, y_ref, x_s, st_s, *, ns):
    d = pl.program_id(0)
    c = pl.program_id(2)
    lc, bs, cu = u_ref.shape

    @pl.when(c == 0)
    def _():
        st_s[...] = jnp.zeros_like(st_s)

    u = u_ref[...].reshape(lc * bs, cu)
    x_s[...] = _bdot(u, win_ref[...])
    are = jnp.broadcast_to(are_ref[...], (bs, ns))
    aim = jnp.broadcast_to(aim_ref[...], (bs, ns))

    def body(i, carry):
        xr, xi = carry
        t = jnp.where(d == 0, i, lc - 1 - i)
        off = pl.multiple_of(t * bs, bs)
        ir = x_s[pl.ds(off, bs), 0:ns]
        ii = x_s[pl.ds(off, bs), ns:2 * ns]
        nr = are * xr - aim * xi + ir
        ni = are * xi + aim * xr + ii
        x_s[pl.ds(off, bs), 0:ns] = nr
        x_s[pl.ds(off, bs), ns:2 * ns] = ni
        return nr, ni

    xr, xi = lax.fori_loop(0, lc, body, (st_s[0], st_s[1]))
    st_s[0] = xr
    st_s[1] = xi
    y = _bdot(x_s[...], cout_ref[...])
    y_ref[...] = y.reshape(lc, bs, y_ref.shape[-1])


def _s5_scan(u3, sw, lc=128):
    L, B, cu = u3.shape
    bs = V7X_SUBLANES
    ns = sw["ns"]
    nl = L // lc

    def tix(d, c):
        return jnp.where(d == 0, c, nl - 1 - c)

    return pl.pallas_call(
        functools.partial(_s5_scan_kernel, ns=ns),
        grid=(2, B // bs, nl),
        in_specs=[pl.BlockSpec((lc, bs, cu), lambda d, b, c: (tix(d, c), b, 0)),
                  pl.BlockSpec((None, cu, 2 * ns), lambda d, b, c: (d, 0, 0)),
                  pl.BlockSpec((None, 1, ns), lambda d, b, c: (d, 0, 0)),
                  pl.BlockSpec((None, 1, ns), lambda d, b, c: (d, 0, 0)),
                  pl.BlockSpec((2 * ns, cu), lambda d, b, c: (0, 0))],
        out_specs=pl.BlockSpec((None, lc, bs, cu), lambda d, b, c: (d, tix(d, c), b, 0)),
        out_shape=SDS((2, L, B, cu), F32),
        scratch_shapes=[pltpu.VMEM((lc * bs, 2 * ns), F32), pltpu.VMEM((2, bs, ns), F32)],
        compiler_params=_cparams(("parallel", "parallel", "arbitrary")), name="s5_scan")(
            u3, sw["win"], sw["a_re"], sw["a_im"], sw["cout"])


def _gelu_tanh(x):
    return 0.5 * x * (1.0 + jnp.tanh(math.sqrt(2.0 / math.pi) * (x + 0.044715 * (x * x * x))))


def _s5_post_kernel(y_ref, u_ref, d_ref, wg_ref, g_ref, o_ref):
    u = u_ref[...]
    y = y_ref[0] + y_ref[1] + d_ref[...] * u
    h = _gelu_tanh(y)
    out = h * _sigmoid(_bdot(h, wg_ref[...]))
    o_ref[...] = _rms(out, g_ref[...])


def _s5_post(y2, u_tm, B, L, sw, tl=512):
    cu = sw["cu"]
    nl = L // tl
    const = lambda b, l: (0, 0)
    return pl.pallas_call(
        _s5_post_kernel,
        grid=(B, nl),
        in_specs=[pl.BlockSpec((2, tl, cu), lambda b, l: (0, l, b)),
                  pl.BlockSpec((tl, cu), lambda b, l: (l, b)),
                  pl.BlockSpec((1, cu), const), pl.BlockSpec((cu, cu), const), pl.BlockSpec((1, cu), const)],
        out_specs=pl.BlockSpec((tl, cu), lambda b, l: (b * nl + l, 0)),
        out_shape=SDS((B * L, cu), F32),
        compiler_params=_cparams(("parallel", "parallel")), name="s5_post")(
            y2.reshape(2, L, B * cu), u_tm, sw["d"], sw["w_glu"], sw["norm_g"])


def _s5_mixer(u_tm, B, L, sw):
    cu = sw["cu"]
    y2 = _s5_scan(u_tm.reshape(L, B, cu), sw)
    return _s5_post(y2, u_tm, B, L, sw)


def _hy_pre_kernel(z_ref, zp_ref, zn_ref, cw_ref, cb_ref, v_o, x1_o, x2_o, *, dh):
    z = z_ref[...]
    prev_row, next_row = _halo_rows(zp_ref, zn_ref)
    zprev, znext = _shift_rows(z, prev_row, next_row)
    zc = cw_ref[0:1, :] * zprev + cw_ref[1:2, :] * z + cw_ref[2:3, :] * znext + cb_ref[...]
    v_o[...] = zc[:, 0:dh]
    x1_o[...] = zc[:, dh:2 * dh]
    x2_o[...] = zc[:, 2 * dh:3 * dh]


def _hy_pre(z, B, L, hw, tm=512):
    T, ch = z.shape
    dh = hw["dh"]
    nl = L // tm
    row = lambda b, l: (b * nl + l, 0)
    const = lambda b, l: (0, 0)
    prev, nxt = _halo_specs(tm, ch, nl, T)
    o = pl.BlockSpec((tm, dh), row)
    return pl.pallas_call(
        functools.partial(_hy_pre_kernel, dh=dh),
        grid=(B, nl),
        in_specs=[pl.BlockSpec((tm, ch), row), prev, nxt,
                  pl.BlockSpec((3, ch), const), pl.BlockSpec((1, ch), const)],
        out_specs=[o, o, o],
        out_shape=[SDS((T, dh), F32)] * 3,
        compiler_params=_cparams(("parallel", "parallel")), name="hy_pre")(z, z, z, hw["conv_w"], hw["conv_b"])


def _hy_filter_kernel(ft_ref, w1_ref, b1_ref, w2_ref, b2_ref, w3_ref, b3_ref, fr_ref, dl_ref, bm_ref,
                      h_o, s_o, *, L, tl):
    i = pl.program_id(0)
    fr = fr_ref[...]
    h = jnp.sin(fr * (_dot3(ft_ref[...], w1_ref[...]) + b1_ref[...]))
    h = jnp.sin(fr * (_dot3(h, w2_ref[...]) + b2_ref[...]))
    h = _dot3(h, w3_ref[...]) + b3_ref[...]
    pos = (i * tl + lax.broadcasted_iota(I32, (tl, 1), 0)).astype(F32)
    t = pos * (1.0 / (L - 1))
    h = h * jnp.exp(-t * dl_ref[...])
    h_o[...] = h
    keep = jnp.where((pos == 0.0) & (bm_ref[...] > 0.5), 0.0, 1.0)
    part = jnp.sum(jnp.abs(h) * keep, axis=0, keepdims=True)

    @pl.when(i == 0)
    def _():
        s_o[...] = jnp.zeros_like(s_o)

    s_o[...] += part


def _hy_filter(L, hw, tl=256):
    feats = _hyena_features(L)
    ncol = hw["f_w3"].shape[1]
    fw = hw["f_w1"].shape[1]
    const = lambda i: (0, 0)
    full = lambda a: pl.BlockSpec(a.shape, const)
    args = [hw["f_w1"], hw["f_b1"], hw["f_w2"], hw["f_b2"], hw["f_w3"], hw["f_b3"], hw["f_freq"],
            hw["abs_deltas"], hw["bwd_mask"]]
    return pl.pallas_call(
        functools.partial(_hy_filter_kernel, L=L, tl=tl),
        grid=(L // tl,),
        in_specs=[pl.BlockSpec((tl, feats.shape[1]), lambda i: (i, 0))] + [full(a) for a in args],
        out_specs=[pl.BlockSpec((tl, ncol), lambda i: (i, 0)), pl.BlockSpec((1, ncol), const)],
        out_shape=[SDS((L, ncol), F32), SDS((1, ncol), F32)],
        compiler_params=_cparams(("arbitrary",)), name="hy_filter")(feats, *args)


def _hyena_features(L):
    emb_bands = 16
    t = jnp.linspace(0.0, 1.0, L, dtype=F32)[:, None]
    w = (2.0 * math.pi / L) * jnp.arange(L, dtype=F32)[:, None]
    f = jnp.linspace(1e-4, emb_bands - 1, emb_bands, dtype=F32)[None, :]
    feats = jnp.concatenate([t, jnp.cos(f * w), -jnp.sin(f * w)], axis=-1)
    return jnp.pad(feats, ((0, 0), (0, V7X_LANES - feats.shape[1])))


def _dft_tables(N):
    n2 = DFT_INNER
    n1 = N // n2
    two_pi = 2.0 * math.pi

    def cs(num, den):
        ang = (num % den).astype(F32) * (two_pi / den)
        return jnp.cos(ang), jnp.sin(ang)

    f1 = jnp.arange(n1, dtype=I32)
    c1, s1 = cs(f1[:, None] * f1[None, :], n1)
    fwd_full = jnp.concatenate([jnp.concatenate([c1, s1], 1), jnp.concatenate([-s1, c1], 1)], 0)
    h = n1 // 2
    fwd_half = jnp.concatenate([jnp.concatenate([c1[:, :h], s1[:, :h]], 1),
                                jnp.concatenate([-s1[:, :h], c1[:, :h]], 1)], 0)
    ci, si = c1[:h, :], s1[:h, :]
    inv_half = jnp.concatenate([jnp.concatenate([ci, -si], 1), jnp.concatenate([si, ci], 1)], 0) * (1.0 / N)
    f2 = jnp.arange(n2, dtype=I32)
    freq = f1[:, None, None] + n1 * f2[None, :, None]
    c2, s2 = cs(freq * f2[None, None, :], N)
    b_fwd = jnp.concatenate([jnp.concatenate([c2, s2], 2), jnp.concatenate([-s2, c2], 2)], 1)
    b_inv = jnp.swapaxes(b_fwd, 1, 2)
    return dict(n1=n1, n2=n2, fwd_full=fwd_full, fwd_half=fwd_half, inv_half=inv_half, b_fwd=b_fwd, b_inv=b_inv)


def _lead_kernel(*refs, has_inv, has_fwd, has_scale, precise):
    refs = list(refs)
    mm = _dot3 if precise else _bdot
    if has_inv:
        yp_ref, inv_ref, xga_ref, xgb_ref, ua_ref, ub_ref, bias_ref = refs[:7]
        refs = refs[7:]
        yp = yp_ref[...]
        ystk = yp.reshape(yp.shape[0] * yp.shape[1], yp.shape[2])
        y = mm(inv_ref[...], ystk)
        k = y.shape[0] // 2
        bias = bias_ref[...]
        za = xga_ref[...] * (y[:k] + bias * ua_ref[...])
        zb = xgb_ref[...] * (y[k:] + bias * ub_ref[...])
    else:
        xa_ref, xb_ref = refs[:2]
        refs = refs[2:]
        za, zb = xa_ref[...], xb_ref[...]
        if has_scale:
            sc = refs[0][...]
            refs = refs[1:]
            za, zb = za * sc, zb * sc
    if has_fwd:
        fwd_ref = refs[0]
        refs = refs[1:]
    if has_inv:
        refs[0][...] = za
        refs[1][...] = zb
        refs = refs[2:]
    if has_fwd:
        y_o = refs[0]
        yy = mm(fwd_ref[...], jnp.concatenate([za, zb], axis=0))
        y_o[...] = yy.reshape(y_o.shape).astype(y_o.dtype)


def _lead_call(*, pairs, n1, kin, lanes, wl, inv=None, fwd=None, xa=None, xb_off=None, scale=None,
               gate=None, precise=False, name="hy_lead"):
    nj = lanes // wl
    in_specs, args = [], []
    out_specs, out_shape = [], []
    has_inv = gate is not None
    if has_inv:
        kh = gate["xg"].shape[1]
        off = gate["off"]
        sa = pl.BlockSpec((None, kh, wl), lambda p, j: (p, 0, j))
        sb = pl.BlockSpec((None, kh, wl), lambda p, j: (p + off, 0, j))
        in_specs += [pl.BlockSpec((None, 2, n1, wl), lambda p, j: (p, 0, 0, j)),
                     pl.BlockSpec(inv.shape, lambda p, j: (0, 0)), sa, sb, sa, sb,
                     pl.BlockSpec((1, wl), lambda p, j: (0, j))]
        args += [gate["yp"], inv, gate["xg"], gate["xg"], gate["u"], gate["u"], gate["bias"]]
    else:
        in_specs += [pl.BlockSpec((None, kin, wl), lambda p, j: (p, 0, j)),
                     pl.BlockSpec((None, kin, wl), lambda p, j: (p + xb_off, 0, j))]
        args += [xa, xa]
        if scale is not None:
            in_specs.append(pl.BlockSpec((1, wl), lambda p, j: (0, j)))
            args.append(scale)
    if fwd is not None:
        in_specs.append(pl.BlockSpec(fwd.shape, lambda p, j: (0, 0)))
        args.append(fwd)
    if has_inv:
        out_specs += [sa, sa]
        out_shape += [SDS((pairs, kh, lanes), F32)] * 2
    if fwd is not None:
        out_specs.append(pl.BlockSpec((None, 2, n1, wl), lambda p, j: (p, 0, 0, j)))
        out_shape.append(SDS((pairs, 2, n1, lanes), F32 if precise else BF16))
    return pl.pallas_call(
        functools.partial(_lead_kernel, has_inv=has_inv, has_fwd=fwd is not None,
                          has_scale=scale is not None, precise=precise),
        grid=(pairs, nj), in_specs=in_specs, out_specs=out_specs, out_shape=out_shape,
        compiler_params=_cparams(("parallel", "parallel")), name=name)(*args)


def _inner_kernel(y_ref, bf_ref, *rest, f1t, n2, has_inv, precise):
    mm = _dot3 if precise else _bdot
    if has_inv:
        k_ref, bi_ref, o_ref = rest
    else:
        (o_ref,) = rest
    for i in range(f1t):
        ys = jnp.concatenate([y_ref[0, i], y_ref[1, i]], axis=0)
        x = mm(bf_ref[i], ys)
        xr, xi = x[:n2], x[n2:]
        if has_inv:
            kr, ki = k_ref[0, i], k_ref[1, i]
            zr = xr * kr - xi * ki
            zi = xr * ki + xi * kr
            yp = mm(bi_ref[i], jnp.concatenate([zr, zi], axis=0))
            o_ref[0, i] = yp[:n2].astype(o_ref.dtype)
            o_ref[1, i] = yp[n2:].astype(o_ref.dtype)
        else:
            o_ref[0, i] = xr
            o_ref[1, i] = xi


def _inner_call(y5, b_fwd, b_inv=None, kf=None, kblk=0, precise=False, f1t=8, name="hy_inner"):
    P, _, n1, n2, ch = y5.shape
    has_inv = kf is not None
    ys = pl.BlockSpec((None, 2, f1t, n2, ch), lambda j, p: (p, 0, j, 0, 0))
    ts = pl.BlockSpec((f1t, 2 * n2, 2 * n2), lambda j, p: (j, 0, 0))
    in_specs = [ys, ts]
    args = [y5, b_fwd]
    if has_inv:
        in_specs += [pl.BlockSpec((2, f1t, n2, ch), lambda j, p: (0, j, 0, kblk)), ts]
        args += [kf, b_inv]
    return pl.pallas_call(
        functools.partial(_inner_kernel, f1t=f1t, n2=n2, has_inv=has_inv, precise=precise),
        grid=(n1 // f1t, P), in_specs=in_specs, out_specs=ys, out_shape=SDS(y5.shape, y5.dtype),
        compiler_params=_cparams(("parallel", "parallel")), name=name)(*args)


def _hy_spectrum(L, hw, tabs):
    N = 2 * L
    n1, n2 = tabs["n1"], tabs["n2"]
    dh = hw["dh"]
    hwin, asum = _hy_filter(L, hw)
    ks, ss = [], []
    for o in range(2):
        kf = hwin[:, (2 * o) * dh:(2 * o + 1) * dh]
        kb = hwin[:, (2 * o + 1) * dh:(2 * o + 2) * dh]
        ks.append(jnp.concatenate([kf, jnp.zeros((1, dh), F32), kb[:0:-1]], axis=0))
        ss.append(asum[:, (2 * o) * dh:(2 * o + 1) * dh] + asum[:, (2 * o + 1) * dh:(2 * o + 2) * dh])
    k2 = jnp.concatenate(ks, axis=1)
    inv_mass = 1.0 / jnp.concatenate(ss, axis=1)
    lanes = n2 * 2 * dh
    kin = jnp.stack([k2.reshape(n1, lanes), jnp.zeros((n1, lanes), F32)])
    (y,) = _lead_call(pairs=1, n1=n1, kin=n1, lanes=lanes, wl=2048, fwd=tabs["fwd_full"], xa=kin, xb_off=1,
                      scale=jnp.tile(inv_mass, (1, n2)), precise=True, name="hy_filter_lead")
    kf = _inner_call(y.reshape(1, 2, n1, n2, 2 * dh), tabs["b_fwd"], precise=True, name="hy_filter_inner")
    return kf[0]


def _hyena_mixer(z, B, L, hw, tabs, kf):
    dh = hw["dh"]
    n1, n2 = tabs["n1"], tabs["n2"]
    kh = n1 // 2
    lanes = n2 * dh
    P = B // 2
    wl = 2048
    v, x1, x2 = _hy_pre(z, B, L, hw)
    view = lambda a: a.reshape(B, kh, lanes)
    v3, x13, x23 = view(v), view(x1), view(x2)
    bias = [jnp.tile(hw["bias"][o:o + 1], (1, n2)) for o in range(2)]
    (y,) = _lead_call(pairs=P, n1=n1, kin=kh, lanes=lanes, wl=wl, fwd=tabs["fwd_half"], xa=v3, xb_off=P,
                      name="hy_lead0")
    yp = _inner_call(y.reshape(P, 2, n1, n2, dh), tabs["b_fwd"], tabs["b_inv"], kf, kblk=0, name="hy_inner0")
    z1a, z1b, y = _lead_call(pairs=P, n1=n1, kin=kh, lanes=lanes, wl=wl, inv=tabs["inv_half"], fwd=tabs["fwd_half"],
                             gate=dict(yp=yp.reshape(P, 2, n1, lanes), xg=x13, u=v3, bias=bias[0], off=P),
                             name="hy_lead1")
    z1 = jnp.concatenate([z1a, z1b], axis=0)
    yp = _inner_call(y.reshape(P, 2, n1, n2, dh), tabs["b_fwd"], tabs["b_inv"], kf, kblk=1, name="hy_inner1")
    ya, yb = _lead_call(pairs=P, n1=n1, kin=kh, lanes=lanes, wl=wl, inv=tabs["inv_half"],
                        gate=dict(yp=yp.reshape(P, 2, n1, lanes), xg=x23, u=z1, bias=bias[1], off=P),
                        name="hy_lead2")
    return jnp.concatenate([ya, yb], axis=0).reshape(B * L, dh)


def _outproj_kernel(yr_ref, ys_ref, yh_ref, x_ref, wo_ref, gh_ref, gf_ref, wr_ref, xo_ref, xn_ref, aff_ref,
                    *, dr, ds):
    yh = _rms(yh_ref[...], gh_ref[...])
    acc = _bdot(yr_ref[...], wo_ref[0:dr, :])
    acc += _bdot(ys_ref[...], wo_ref[dr:dr + ds, :])
    acc += _bdot(yh, wo_ref[dr + ds:, :])
    xnew = x_ref[...] + acc
    xo_ref[...] = xnew
    xn = _rms(xnew, gf_ref[...])
    xn_ref[...] = xn
    logits = _dot3_t(wr_ref[...], xn)
    m = jnp.max(logits, axis=0, keepdims=True)
    e = jnp.exp(logits - m)
    aff_ref[...] = e / jnp.sum(e, axis=0, keepdims=True)


def _out_proj(yr, ys, yh, x, lw, tm=512):
    T, D = x.shape
    dr, ds, dh = yr.shape[1], ys.shape[1], yh.shape[1]
    E = lw["w_router_t"].shape[0]
    row = lambda i: (i, 0)
    const = lambda i: (0, 0)
    return pl.pallas_call(
        functools.partial(_outproj_kernel, dr=dr, ds=ds),
        grid=(T // tm,),
        in_specs=[pl.BlockSpec((tm, dr), row), pl.BlockSpec((tm, ds), row), pl.BlockSpec((tm, dh), row),
                  pl.BlockSpec((tm, D), row), pl.BlockSpec((D, D), const), pl.BlockSpec((1, dh), const),
                  pl.BlockSpec((1, D), const), pl.BlockSpec((E, D), const)],
        out_specs=[pl.BlockSpec((tm, D), row), pl.BlockSpec((tm, D), row), pl.BlockSpec((E, tm), lambda i: (0, i))],
        out_shape=[SDS((T, D), F32), SDS((T, D), F32), SDS((E, T), F32)],
        compiler_params=_cparams(("parallel",)), name="out_proj")(
            yr, ys, yh, x, lw["w_out"], lw["hy"]["norm_g"], lw["norm_ffn_g"], lw["w_router_t"])


def _select_prefix_kernel(aff_ref, incl_ref, off_ref, *, cap):
    aff = aff_ref[...]
    nt = aff.shape[0]
    bits = pltpu.bitcast(aff, I32)

    def body(i, thr):
        cand = thr | jnp.left_shift(jnp.int32(1), 30 - i)
        cnt = jnp.sum(jnp.where(bits >= cand, 1.0, 0.0))
        return jnp.where(cnt >= cap, cand, thr)

    thr = lax.fori_loop(0, 31, body, jnp.int32(0))
    gt = bits > thr
    eq = bits == thr
    need = cap - jnp.sum(jnp.where(gt, 1.0, 0.0))

    li = lax.broadcasted_iota(I32, (V7X_LANES, V7X_LANES), 0)
    mi = lax.broadcasted_iota(I32, (V7X_LANES, V7X_LANES), 1)
    upper = jnp.where(li <= mi, 1.0, 0.0).astype(BF16)
    ones = jnp.ones((V7X_LANES, V7X_LANES), BF16)
    ri = lax.broadcasted_iota(I32, (nt, nt), 0)
    ci = lax.broadcasted_iota(I32, (nt, nt), 1)
    lower = jnp.where(ci < ri, 1.0, 0.0).astype(BF16)
    dot = functools.partial(jnp.dot, preferred_element_type=F32)

    def prefix(m):
        mb = m.astype(BF16)
        incl = dot(mb, upper)
        tot = dot(mb, ones)
        return incl, dot(lower, tot.astype(BF16))

    eqf = jnp.where(eq, 1.0, 0.0)
    incl_eq, off_eq = prefix(eqf)
    rank_eq = off_eq + incl_eq - eqf
    sel = jnp.where(gt, 1.0, jnp.where(eq & (rank_eq < need), 1.0, 0.0))
    incl, off = prefix(sel)
    incl_ref[...] = incl
    off_ref[...] = off


def _select_index_kernel(incl_ref, off_ref, aff_ref, idx_ref, gate_ref, *, sb):
    j0 = pl.program_id(1) * sb
    incl = incl_ref[...]
    off = off_ref[...]
    nt = incl.shape[0]
    off_col = off[:, 0:1]
    end_col = off_col + incl[:, V7X_LANES - 1:V7X_LANES]
    j = (j0 + lax.broadcasted_iota(I32, (1, sb), 1)).astype(F32)
    tile_id = jnp.sum(jnp.where(end_col <= j, 1.0, 0.0), axis=0, keepdims=True)
    ti = lax.broadcasted_iota(I32, (nt, sb), 0).astype(F32)
    onehot = jnp.where(ti == tile_id, 1.0, 0.0)
    excl = jnp.sum(onehot * off_col, axis=0, keepdims=True)
    rj = j - excl
    ohb = onehot.astype(BF16)
    dot = functools.partial(jnp.dot, preferred_element_type=F32)
    rows = dot(incl.T.astype(BF16), ohb)
    local = jnp.sum(jnp.where(rows <= rj, 1.0, 0.0), axis=0, keepdims=True)
    idx_ref[...] = (tile_id * V7X_LANES + local).astype(I32)
    ah, am, al = _split3(aff_ref[...].T)
    arows = dot(ah, ohb) + dot(am, ohb) + dot(al, ohb)
    lane = lax.broadcasted_iota(I32, (V7X_LANES, sb), 0).astype(F32)
    gate_ref[...] = jnp.sum(jnp.where(lane == local, arows, 0.0), axis=0, keepdims=True)


def _moe_select(aff_t, cap, sb=1024):
    E, T = aff_t.shape
    nt = T // V7X_LANES
    aff3 = aff_t.reshape(E, nt, V7X_LANES)
    s3 = pl.BlockSpec((None, nt, V7X_LANES), lambda e: (e, 0, 0))
    incl, off = pl.pallas_call(
        functools.partial(_select_prefix_kernel, cap=cap),
        grid=(E,), in_specs=[s3], out_specs=[s3, s3],
        out_shape=[SDS((E, nt, V7X_LANES), F32)] * 2,
        compiler_params=_cparams(("parallel",)), name="moe_select_prefix")(aff3)
    s3b = pl.BlockSpec((None, nt, V7X_LANES), lambda e, j: (e, 0, 0))
    so = pl.BlockSpec((None, 1, sb), lambda e, j: (e, 0, j))
    idx, gate = pl.pallas_call(
        functools.partial(_select_index_kernel, sb=sb),
        grid=(E, cap // sb), in_specs=[s3b, s3b, s3b], out_specs=[so, so],
        out_shape=[SDS((E, 1, cap), I32), SDS((E, 1, cap), F32)],
        compiler_params=_cparams(("parallel", "parallel")), name="moe_select_index")(incl, off, aff3)
    return idx, gate


def _moe_ffn_kernel(idx_hbm, gate_ref, xn_hbm, acc_in_hbm, wg_ref, wu_ref, wd_ref, acc_hbm,
                    idx_s, xbuf, abuf, sem_i, sem_x, sem_a, sem_s, *, m, nt):
    del acc_in_hbm
    hm = m // 2
    cp = pltpu.make_async_copy(idx_hbm.at[pl.program_id(0) * nt + pl.program_id(1)], idx_s, sem_i)
    cp.start()
    cp.wait()

    def rows(k):
        return pl.ds(k * hm, hm)

    for k in range(2):
        def issue(r, carry, k=k):
            t = idx_s[k * hm + r]
            pltpu.make_async_copy(xn_hbm.at[t], xbuf.at[k * hm + r], sem_x.at[k]).start(priority=0)
            pltpu.make_async_copy(acc_hbm.at[t], abuf.at[k * hm + r], sem_a.at[k]).start(priority=1)
            return carry

        lax.fori_loop(0, hm, issue, 0, unroll=8)

    g_col = jnp.broadcast_to(gate_ref[...], (V7X_LANES, m)).T[:, 0:1]
    for k in range(2):
        pltpu.make_async_copy(xn_hbm.at[pl.ds(0, hm)], xbuf.at[rows(k)], sem_x.at[k]).wait()
        x = xbuf[rows(k), :].astype(BF16)
        hg = jnp.dot(x, wg_ref[...], preferred_element_type=F32)
        hu = jnp.dot(x, wu_ref[...], preferred_element_type=F32)
        h = hg * _sigmoid(hg) * hu
        y = jnp.dot(h.astype(BF16), wd_ref[...], preferred_element_type=F32)
        pltpu.make_async_copy(acc_hbm.at[pl.ds(0, hm)], abuf.at[rows(k)], sem_a.at[k]).wait()
        abuf[rows(k), :] = abuf[rows(k), :] + y * g_col[k * hm:(k + 1) * hm]

        def scatter(r, carry, k=k):
            for q in range(2):
                row = k * hm + 2 * r + q
                pltpu.make_async_copy(abuf.at[row], acc_hbm.at[idx_s[row]], sem_s.at[k]).start(priority=q)
            return carry

        lax.fori_loop(0, hm // 2, scatter, 0, unroll=4)

    for k in range(2):
        pltpu.make_async_copy(abuf.at[rows(k)], acc_hbm.at[pl.ds(0, hm)], sem_s.at[k]).wait()


def _moe_ffn(idx, gate, xn, acc, mw, m=1024):
    T, D = xn.shape
    E, _, cap = idx.shape
    nt = cap // m
    F = mw["w_gate"].shape[2]
    idx2 = idx.reshape(E * nt, m)
    gate3 = gate.reshape(E * nt, 1, m)
    wspec = lambda a, b: pl.BlockSpec((None, a, b), lambda e, i: (e, 0, 0))
    any_spec = pl.BlockSpec(memory_space=pl.ANY)
    return pl.pallas_call(
        functools.partial(_moe_ffn_kernel, m=m, nt=nt),
        grid=(E, nt),
        in_specs=[any_spec, pl.BlockSpec((None, 1, m), lambda e, i: (e * nt + i, 0, 0)), any_spec, any_spec,
                  wspec(D, F), wspec(D, F), wspec(F, D)],
        out_specs=any_spec,
        out_shape=SDS((T, D), F32),
        input_output_aliases={3: 0},
        scratch_shapes=[pltpu.SMEM((m,), I32), pltpu.VMEM((m, D), F32), pltpu.VMEM((m, D), F32),
                        pltpu.SemaphoreType.DMA, pltpu.SemaphoreType.DMA((2,)), pltpu.SemaphoreType.DMA((2,)),
                        pltpu.SemaphoreType.DMA((2,))],
        compiler_params=_cparams(("arbitrary", "arbitrary")), name="moe_ffn")(
            idx2, gate3, xn, acc, mw["w_gate"], mw["w_up"], mw["w_down"])


def _moe(xnew, xn, aff_t, mw):
    T = xn.shape[0]
    E = aff_t.shape[0]
    cap = EC_CAPACITY_FACTOR * T // E
    idx, gate = _moe_select(aff_t, cap)
    return _moe_ffn(idx, gate, xn, xnew, mw)


def _norm_kernel(x_ref, g_ref, o_ref):
    o_ref[...] = _rms(x_ref[...], g_ref[...])


def _final_norm(x, g, tm=512):
    T, D = x.shape
    return pl.pallas_call(
        _norm_kernel, grid=(T // tm,),
        in_specs=[pl.BlockSpec((tm, D), lambda i: (i, 0)), pl.BlockSpec((1, D), lambda i: (0, 0))],
        out_specs=pl.BlockSpec((tm, D), lambda i: (i, 0)), out_shape=SDS((T, D), F32),
        compiler_params=_cparams(("parallel",)), name="final_norm")(x, g)


def _block_diag_rows(blocks):
    n, r, c = blocks.shape
    eye = jnp.eye(n, dtype=blocks.dtype)
    return jnp.einsum("nrc,nm->nrmc", blocks, eye).reshape(n * r, n * c)


def _rwkv_weights(i, p):
    heads, hd = p["rwkv_r_k"].shape[1:]
    dr = heads * hd
    row = lambda a: a.reshape(1, -1)
    head_id = jnp.arange(dr) // hd
    return dict(
        dr=dr, heads=heads, hd=hd,
        mu_prev=row(p["rwkv_mu_prev"][i]), mu_next=row(p["rwkv_mu_next"][i]),
        w0=row(p["rwkv_w0"][i]), wup=_block_diag_rows(p["rwkv_w_up"][i]).astype(BF16),
        a0=row(p["rwkv_a0"][i]), aup=_block_diag_rows(p["rwkv_a_up"][i]).astype(BF16),
        gup=p["rwkv_g_up"][i].astype(BF16),
        k_k=row(p["rwkv_k_k"][i]), k_a=row(p["rwkv_k_a"][i]), r_k=row(p["rwkv_r_k"][i]),
        ln_w=row(p["rwkv_ln_w"][i]), ln_b=row(p["rwkv_ln_b"][i]),
        hsum=(head_id[:, None] == head_id[None, :]).astype(BF16))


def _s5_weights(i, p):
    lam_re, lam_im, log_dt = p["s5_lam_re"][i], p["s5_lam_im"][i], p["s5_log_dt"][i]
    b_re, b_im, c_re, c_im = p["s5_b_re"][i], p["s5_b_im"][i], p["s5_c_re"][i], p["s5_c_im"][i]
    G, P, H = b_re.shape
    l_re = jnp.minimum(lam_re, -1e-4)
    dt = jnp.exp(log_dt)[..., None]
    mag = jnp.exp(l_re * dt)
    ab_re = mag * jnp.cos(lam_im * dt)
    ab_im = mag * jnp.sin(lam_im * dt)
    den = l_re * l_re + lam_im * lam_im
    n_re = ab_re - 1.0
    f_re = (n_re * l_re + ab_im * lam_im) / den
    f_im = (ab_im * l_re - n_re * lam_im) / den
    w_re = jnp.swapaxes(f_re[..., None] * b_re - f_im[..., None] * b_im, 2, 3)
    w_im = jnp.swapaxes(f_re[..., None] * b_im + f_im[..., None] * b_re, 2, 3)
    win = jnp.stack([jnp.concatenate([_block_diag_rows(w_re[d]), _block_diag_rows(w_im[d])], axis=1)
                     for d in range(2)]).astype(BF16)
    cout = jnp.concatenate([_block_diag_rows(jnp.swapaxes(c_re, 1, 2)),
                            _block_diag_rows(jnp.swapaxes(-c_im, 1, 2))], axis=0).astype(BF16)
    ns = G * P
    return dict(ns=ns, cu=G * H, win=win, cout=cout,
                a_re=ab_re.reshape(2, 1, ns), a_im=ab_im.reshape(2, 1, ns),
                d=p["s5_d"][i].reshape(1, -1), w_glu=p["s5_w_glu"][i].astype(BF16),
                norm_g=p["s5_norm_g"][i].reshape(1, -1))


def _hyena_weights(i, p):
    dh = p["hy_norm_g"].shape[1]
    emb = p["hy_f_w1"].shape[1]
    w1 = jnp.pad(p["hy_f_w1"][i], ((0, V7X_LANES - emb), (0, 0)))
    deltas = jnp.linspace(math.log(HYENA_TARGET) / HYENA_FAST_DECAY, math.log(HYENA_TARGET) / HYENA_SLOW_DECAY,
                          dh, dtype=F32)
    ncol = p["hy_f_w3"].shape[2]
    col_dir = (jnp.arange(ncol) // dh) % 2
    row = lambda a: a.reshape(1, -1)
    return dict(dh=dh, conv_w=p["hy_conv_w"][i], conv_b=row(p["hy_conv_b"][i]),
                f_w1=w1, f_b1=row(p["hy_f_b1"][i]), f_w2=p["hy_f_w2"][i], f_b2=row(p["hy_f_b2"][i]),
                f_w3=p["hy_f_w3"][i], f_b3=row(p["hy_f_b3"][i]), f_freq=row(p["hy_f_freq"][i]),
                abs_deltas=jnp.tile(jnp.abs(deltas), ncol // dh).reshape(1, -1),
                bwd_mask=col_dir.astype(F32).reshape(1, -1),
                bias=p["hy_bias"][i], norm_g=row(p["hy_norm_g"][i]))


def _layer_weights(i, p):
    return dict(
        norm_mix_g=p["norm_mix_g"][i].reshape(1, -1), w_in=p["w_in"][i].astype(BF16),
        rwkv=_rwkv_weights(i, p), s5=_s5_weights(i, p), hy=_hyena_weights(i, p),
        w_out=p["w_out"][i].astype(BF16), norm_ffn_g=p["norm_ffn_g"][i].reshape(1, -1),
        w_router_t=p["moe_w_router"][i].T,
        moe=dict(w_gate=p["moe_w_gate"][i].astype(BF16), w_up=p["moe_w_up"][i].astype(BF16),
                 w_down=p["moe_w_down"][i].astype(BF16)))


def _layer(x, B, L, lw, tabs, kf):
    c_r = lw["rwkv"]["mu_prev"].shape[1]
    c_s = lw["s5"]["cu"]
    c_h = 3 * lw["hy"]["dh"]
    z_r, z_s, z_h = _in_proj(x, B, L, lw["norm_mix_g"], lw["w_in"], c_r, c_s, c_h)
    y_r = _rwkv_mixer(z_r, B, L, lw["rwkv"])
    y_s = _s5_mixer(z_s, B, L, lw["s5"])
    y_h = _hyena_mixer(z_h, B, L, lw["hy"], tabs, kf)
    xnew, xn, aff_t = _out_proj(y_r, y_s, y_h, x, lw)
    return _moe(xnew, xn, aff_t, lw["moe"])


def _trunk(x, layers, final_g):
    B, L, D = x.shape
    tabs = _dft_tables(2 * L)
    h = x.reshape(B * L, D)
    for lw in layers:
        kf = _hy_spectrum(L, lw["hy"], tabs)
        h = _layer(h, B, L, lw, tabs, kf)
    return _final_norm(h, final_g.reshape(1, -1)).reshape(B, L, D)


def kernel(x_prompt, x_sample, norm_mix_g, w_in, rwkv_mu_prev, rwkv_mu_next, rwkv_w0, rwkv_w_up, rwkv_a0, rwkv_a_up, rwkv_g_up, rwkv_k_k, rwkv_k_a, rwkv_r_k, rwkv_ln_w, rwkv_ln_b, s5_lam_re, s5_lam_im, s5_log_dt, s5_b_re, s5_b_im, s5_c_re, s5_c_im, s5_d, s5_w_glu, s5_norm_g, hy_conv_w, hy_conv_b, hy_f_w1, hy_f_b1, hy_f_w2, hy_f_b2, hy_f_w3, hy_f_b3, hy_f_freq, hy_bias, hy_norm_g, w_out, norm_ffn_g, moe_w_router, moe_w_gate, moe_w_up, moe_w_down, final_norm_g):
    p = dict(norm_mix_g=norm_mix_g, w_in=w_in, rwkv_mu_prev=rwkv_mu_prev, rwkv_mu_next=rwkv_mu_next,
             rwkv_w0=rwkv_w0, rwkv_w_up=rwkv_w_up, rwkv_a0=rwkv_a0, rwkv_a_up=rwkv_a_up,
             rwkv_g_up=rwkv_g_up, rwkv_k_k=rwkv_k_k, rwkv_k_a=rwkv_k_a, rwkv_r_k=rwkv_r_k,
             rwkv_ln_w=rwkv_ln_w, rwkv_ln_b=rwkv_ln_b, s5_lam_re=s5_lam_re, s5_lam_im=s5_lam_im,
             s5_log_dt=s5_log_dt, s5_b_re=s5_b_re, s5_b_im=s5_b_im, s5_c_re=s5_c_re, s5_c_im=s5_c_im,
             s5_d=s5_d, s5_w_glu=s5_w_glu, s5_norm_g=s5_norm_g, hy_conv_w=hy_conv_w, hy_conv_b=hy_conv_b,
             hy_f_w1=hy_f_w1, hy_f_b1=hy_f_b1, hy_f_w2=hy_f_w2, hy_f_b2=hy_f_b2, hy_f_w3=hy_f_w3,
             hy_f_b3=hy_f_b3, hy_f_freq=hy_f_freq, hy_bias=hy_bias, hy_norm_g=hy_norm_g, w_out=w_out,
             norm_ffn_g=norm_ffn_g, moe_w_router=moe_w_router, moe_w_gate=moe_w_gate,
             moe_w_up=moe_w_up, moe_w_down=moe_w_down)
    layers = [_layer_weights(i, p) for i in range(w_in.shape[0])]
    return (_trunk(x_prompt, layers, final_norm_g), _trunk(x_sample, layers, final_norm_g))
```
